```python
import math
import jax
import jax.numpy as jnp
from jax import lax
import numpy as np

D_MODEL = 1024
BATCH = 4
SEQ = 4096
DEPTH = 1
DEC_BATCH = 2
DEC_SEQ = 8192
PAST_LEN = 128

HEAD_DIM = 64
DA_HEADS = 8
DA_VDIM = 2 * HEAD_DIM
DIL_CONFIG = ((128, 1), (512, 4), (2048, 16))
N_DIL = 3
DIL_HEADS = 8
D_FF = 4 * D_MODEL
N_BUCKETS = 32
MAX_DISTANCE = 128
Q_BLOCK = 128
NEG_INF = -1e30
EPS = 1e-6

DA_QK = DA_HEADS * HEAD_DIM
DA_V = DA_HEADS * DA_VDIM
DIL_W = DIL_HEADS * HEAD_DIM
COL_SIZES = (DA_QK, DA_QK, DA_QK, DA_QK, DA_V) + (DIL_W,) * (3 * N_DIL) + (D_MODEL, D_MODEL)
IN_COLS = 4 * DA_QK + DA_V + 3 * N_DIL * DIL_W + 2 * D_MODEL
N_REL_HEADS = DA_HEADS + N_DIL * DIL_HEADS

kernel_name = "hybrid_diff_dilated_encoder"


def rms_norm(x, g):
    xf = x.astype(jnp.float32)
    y = xf * lax.rsqrt(jnp.mean(jnp.square(xf), axis=-1, keepdims=True) + EPS)
    return (y * g.astype(jnp.float32)).astype(x.dtype)


def rel_bucket(rel):
    nb = N_BUCKETS // 2
    max_exact = nb // 2
    ret = jnp.where(rel > 0, nb, 0)
    n = jnp.abs(rel)
    nf = jnp.maximum(n, 1).astype(jnp.float32)
    large = max_exact + (jnp.log(nf / max_exact) / math.log(MAX_DISTANCE / max_exact)
                         * (nb - max_exact)).astype(jnp.int32)
    large = jnp.minimum(large, nb - 1)
    return ret + jnp.where(n < max_exact, n, large)


def diff_attention(q1, q2, k1, k2, v, bias_tab, lam):
    B, H, S, Dh = q1.shape
    nblk = S // Q_BLOCK
    scale = Dh ** -0.5
    kpos = jnp.arange(S)

    def block(args):
        i, q1b, q2b = args
        qpos = i * Q_BLOCK + jnp.arange(Q_BLOCK)
        bias = jnp.transpose(bias_tab[rel_bucket(kpos[None, :] - qpos[:, None])], (2, 0, 1)).astype(jnp.float32)
        s1 = jnp.einsum('bhqd,bhkd->bhqk', q1b, k1).astype(jnp.float32) * scale + bias
        s2 = jnp.einsum('bhqd,bhkd->bhqk', q2b, k2).astype(jnp.float32) * scale + bias
        p = jax.nn.softmax(s1, axis=-1) - lam * jax.nn.softmax(s2, axis=-1)
        return jnp.einsum('bhqk,bhkv->bhqv', p.astype(v.dtype), v)

    qb1 = q1.reshape(B, H, nblk, Q_BLOCK, Dh).transpose(2, 0, 1, 3, 4)
    qb2 = q2.reshape(B, H, nblk, Q_BLOCK, Dh).transpose(2, 0, 1, 3, 4)
    out = lax.map(block, (jnp.arange(nblk), qb1, qb2))
    return out.transpose(1, 0, 3, 2, 4).reshape(B, S, H, v.shape[-1])


def dilated_window_attention(q, k, v, bias_tab, window, dilation):
    B, S, H, Dh = q.shape
    r = dilation
    half = window // (2 * r)
    L = S // r
    blk = half
    nb = -(-L // blk)
    Lp = nb * blk
    scale = Dh ** -0.5

    def to_sub(t):
        return t.reshape(B, L, r, H, Dh).transpose(0, 2, 1, 3, 4)

    qs = jnp.pad(to_sub(q), ((0, 0), (0, 0), (0, Lp - L), (0, 0), (0, 0))).reshape(B, r, nb, blk, H, Dh)

    def windows(t):
        t = jnp.pad(to_sub(t), ((0, 0), (0, 0), (blk, Lp - L + blk), (0, 0), (0, 0))).reshape(B, r, nb + 2, blk, H, Dh)
        return jnp.concatenate([t[:, :, 0:nb], t[:, :, 1:nb + 1], t[:, :, 2:nb + 2]], axis=3)

    kw, vw = windows(k), windows(v)
    qi = jnp.arange(blk)
    kj = jnp.arange(3 * blk) - blk
    rel = kj[None, :] - qi[:, None]
    kpos = jnp.arange(nb)[:, None] * blk + kj[None, :]
    valid = (jnp.abs(rel) <= half)[None] & ((kpos >= 0) & (kpos < L))[:, None, :]
    bias = jnp.transpose(bias_tab[rel_bucket(rel * r)], (2, 0, 1)).astype(jnp.float32)
    s = jnp.einsum('brnqhd,brnkhd->brnhqk', qs, kw).astype(jnp.float32) * scale + bias
    s = jnp.where(valid[:, None], s, NEG_INF)
    lse = jax.nn.logsumexp(s, axis=-1)
    p = jnp.exp(s - lse[..., None])
    out = jnp.einsum('brnhqk,brnkhd->brnqhd', p.astype(v.dtype), vw)
    out = out.reshape(B, r, Lp, H, Dh)[:, :, :L].transpose(0, 2, 1, 3, 4).reshape(B, S, H, Dh)
    lse = lse.transpose(0, 1, 2, 4, 3).reshape(B, r, Lp, H)[:, :, :L].transpose(0, 2, 1, 3).reshape(B, S, H)
    return out, lse


def encoder_layer(x, c, layer_idx, rel_bias, norm1_g, w_ada, b_ada, w_in, qn_a, kn_a,
                  lambda_q1, lambda_k1, lambda_q2, lambda_k2, subln_g, qn_b, kn_b,
                  w_br_a, w_br_b, w_o, norm2_g, w_up, w_down):
    B, S, _ = x.shape
    mod = jax.nn.silu(c) @ w_ada + b_ada
    shift1, scale1, gate1, shift2, scale2, gate2 = [m[:, None, :] for m in jnp.split(mod, 6, axis=-1)]

    h = rms_norm(x, norm1_g) * (1 + scale1) + shift1
    proj = h @ w_in
    splits = []
    acc = 0
    for size in COL_SIZES[:-1]:
        acc += size
        splits.append(acc)
    parts = jnp.split(proj, splits, axis=-1)
    q1, q2, k1, k2, va = parts[:5]
    dil_parts = parts[5:5 + 3 * N_DIL]
    gate_a_logit, gate_b_logit = parts[5 + 3 * N_DIL], parts[6 + 3 * N_DIL]

    def qk_heads(t, g):
        return rms_norm(t.reshape(B, S, -1, HEAD_DIM), g)

    lam_init = 0.8 - 0.6 * math.exp(-0.3 * layer_idx)
    f32 = jnp.float32
    lam = (jnp.exp(jnp.sum(lambda_q1.astype(f32) * lambda_k1.astype(f32)))
           - jnp.exp(jnp.sum(lambda_q2.astype(f32) * lambda_k2.astype(f32))) + lam_init)
    tr = lambda t: t.transpose(0, 2, 1, 3)
    ya = diff_attention(tr(qk_heads(q1, qn_a)), tr(qk_heads(q2, qn_a)),
                        tr(qk_heads(k1, kn_a)), tr(qk_heads(k2, kn_a)),
                        tr(va.reshape(B, S, DA_HEADS, DA_VDIM)), rel_bias[:, :DA_HEADS], lam)
    ya = rms_norm(ya, subln_g) * (1 - lam_init)
    ya = ya.reshape(B, S, DA_V) @ w_br_a

    outs, lses = [], []
    for g, (window, dilation) in enumerate(DIL_CONFIG):
        qg = qk_heads(dil_parts[3 * g], qn_b[g])
        kg = qk_heads(dil_parts[3 * g + 1], kn_b[g])
        vg = dil_parts[3 * g + 2].reshape(B, S, DIL_HEADS, HEAD_DIM)
        tab = rel_bias[:, DA_HEADS + g * DIL_HEADS: DA_HEADS + (g + 1) * DIL_HEADS]
        o, l = dilated_window_attention(qg, kg, vg, tab, window, dilation)
        outs.append(o)
        lses.append(l)
    wts = jax.nn.softmax(jnp.stack(lses, axis=0), axis=0)
    yb = jnp.sum(wts[..., None] * jnp.stack(outs, axis=0).astype(f32), axis=0).astype(x.dtype)
    yb = yb.reshape(B, S, DIL_W) @ w_br_b

    merged = jax.nn.sigmoid(gate_a_logit) * ya + jax.nn.sigmoid(gate_b_logit) * yb
    x = x + gate1 * (merged @ w_o)

    h2 = rms_norm(x, norm2_g) * (1 + scale2) + shift2
    x = x + gate2 * (jnp.square(jax.nn.relu(h2 @ w_up)) @ w_down)
    return x


def trunk(x, c, rel_bias, norm1_g, w_ada, b_ada, w_in, qn_a, kn_a, lambda_q1, lambda_k1,
          lambda_q2, lambda_k2, subln_g, qn_b, kn_b, w_br_a, w_br_b, w_o, norm2_g, w_up, w_down):
    for l in range(DEPTH):
        x = encoder_layer(x, c, l, rel_bias, norm1_g[l], w_ada[l], b_ada[l], w_in[l], qn_a[l], kn_a[l],
                          lambda_q1[l], lambda_k1[l], lambda_q2[l], lambda_k2[l], subln_g[l],
                          qn_b[l], kn_b[l], w_br_a[l], w_br_b[l], w_o[l], norm2_g[l], w_up[l], w_down[l])
    return x


def setup_inputs(seed: int = 0) -> dict:
    key = jax.random.key(seed)
    ks = jax.random.split(key, 24)
    nrm = lambda k, shape, s: jax.random.normal(k, shape, jnp.float32) * s
    D = D_MODEL
    return {
        "x_prompt": nrm(ks[0], (BATCH, SEQ, D), 1.0),
        "x_sample": nrm(ks[1], (DEC_BATCH, DEC_SEQ, D), 1.0),
        "c_prompt": nrm(ks[2], (BATCH, D), 1.0),
        "c_sample": nrm(ks[3], (DEC_BATCH, D), 1.0),
        "rel_bias": nrm(ks[4], (N_BUCKETS, N_REL_HEADS), 0.5),
        "norm1_g": 1.0 + nrm(ks[5], (DEPTH, D), 0.02),
        "w_ada": nrm(ks[6], (DEPTH, D, 6 * D), 0.2 * D ** -0.5),
        "b_ada": nrm(ks[7], (DEPTH, 6 * D), 0.02),
        "w_in": nrm(ks[8], (DEPTH, D, IN_COLS), D ** -0.5),
        "qn_a": 1.0 + nrm(ks[9], (DEPTH, HEAD_DIM), 0.02),
        "kn_a": 1.0 + nrm(ks[10], (DEPTH, HEAD_DIM), 0.02),
        "lambda_q1": nrm(ks[11], (DEPTH, HEAD_DIM), 0.1),
        "lambda_k1": nrm(ks[12], (DEPTH, HEAD_DIM), 0.1),
        "lambda_q2": nrm(ks[13], (DEPTH, HEAD_DIM), 0.1),
        "lambda_k2": nrm(ks[14], (DEPTH, HEAD_DIM), 0.1),
        "subln_g": 1.0 + nrm(ks[15], (DEPTH, DA_VDIM), 0.02),
        "qn_b": 1.0 + nrm(ks[16], (DEPTH, N_DIL, HEAD_DIM), 0.02),
        "kn_b": 1.0 + nrm(ks[17], (DEPTH, N_DIL, HEAD_DIM), 0.02),
        "w_br_a": nrm(ks[18], (DEPTH, DA_V, D), DA_V ** -0.5),
        "w_br_b": nrm(ks[19], (DEPTH, DIL_W, D), DIL_W ** -0.5),
        "w_o": nrm(ks[20], (DEPTH, D, D), D ** -0.5),
        "norm2_g": 1.0 + nrm(ks[21], (DEPTH, D), 0.02),
        "w_up": nrm(ks[22], (DEPTH, D, D_FF), D ** -0.5),
        "w_down": nrm(ks[23], (DEPTH, D_FF, D), D_FF ** -0.5),
    }


def reference(x_prompt, x_sample, c_prompt, c_sample, rel_bias, norm1_g, w_ada, b_ada, w_in,
              qn_a, kn_a, lambda_q1, lambda_k1, lambda_q2, lambda_k2, subln_g, qn_b, kn_b,
              w_br_a, w_br_b, w_o, norm2_g, w_up, w_down):
    y_prompt = trunk(x_prompt, c_prompt, rel_bias, norm1_g, w_ada, b_ada, w_in, qn_a, kn_a,
                     lambda_q1, lambda_k1, lambda_q2, lambda_k2, subln_g, qn_b, kn_b,
                     w_br_a, w_br_b, w_o, norm2_g, w_up, w_down)
    y_sample = trunk(x_sample, c_sample, rel_bias, norm1_g, w_ada, b_ada, w_in, qn_a, kn_a,
                     lambda_q1, lambda_k1, lambda_q2, lambda_k2, subln_g, qn_b, kn_b,
                     w_br_a, w_br_b, w_o, norm2_g, w_up, w_down)
    return (y_prompt, y_sample)
```

```python
import functools
import math

import jax
import jax.numpy as jnp
from jax import lax
from jax.experimental import pallas as pl
from jax.experimental.pallas import tpu as pltpu

F32 = jnp.float32
BF16 = jnp.bfloat16

D_MODEL = 1024
HEAD_DIM = 64
DA_HEADS = 8
DA_VDIM = 2 * HEAD_DIM
DIL_CONFIG = ((128, 1), (512, 4), (2048, 16))
N_DIL = 3
DIL_HEADS = 8
D_FF = 4 * D_MODEL
N_BUCKETS = 32
NEG_INF = -1e30
EPS = 1e-6
LAM_INIT = 0.8 - 0.6 * math.exp(-0.3 * 0)

DA_QK = DA_HEADS * HEAD_DIM
DA_V = DA_HEADS * DA_VDIM
DIL_W = DIL_HEADS * HEAD_DIM
DIL_COLS = 3 * N_DIL * DIL_W
IN_COLS = 4 * DA_QK + DA_V + DIL_COLS + 2 * D_MODEL

LANES = 128
COL_BLK = 512
N_COL_BLKS = IN_COLS // COL_BLK
NORM_W = 256

TM_PROJ = 256
TM_OUT = 256
T_ATT = 512
TQ_DIL = 128
HALF_DIL = 64
W_DIL = TQ_DIL + 2 * HALF_DIL

_BUCKET_STEPS = (12, 16, 23, 32, 46, 64, 91)


def _cparams(sem, vmem_mb):
    return pltpu.CompilerParams(dimension_semantics=sem, vmem_limit_bytes=vmem_mb * 1024 * 1024)


def _mod_kernel(c_ref, w_ref, b_ref, o_ref):
    c = c_ref[...]
    a = c * (1.0 / (1.0 + jnp.exp(-c)))
    o_ref[...] = jnp.dot(a, w_ref[...], preferred_element_type=F32,
                         precision=lax.Precision.HIGHEST) + b_ref[...]


def _modulation(c_all, w_ada, b_ada):
    rows = c_all.shape[0]
    n = w_ada.shape[1]
    return pl.pallas_call(
        _mod_kernel,
        grid=(n // D_MODEL,),
        in_specs=[pl.BlockSpec((rows, D_MODEL), lambda j: (0, 0)),
                  pl.BlockSpec((D_MODEL, D_MODEL), lambda j: (0, j)),
                  pl.BlockSpec((1, D_MODEL), lambda j: (0, j))],
        out_specs=pl.BlockSpec((rows, D_MODEL), lambda j: (0, j)),
        out_shape=jax.ShapeDtypeStruct((rows, n), F32),
        compiler_params=_cparams(("arbitrary",), 32),
    )(c_all, w_ada, b_ada.reshape(1, n))


def _bias_lookup(rel, tab_ref, col):
    n = jnp.abs(rel)
    large = jnp.full(rel.shape, 8, jnp.int32)
    for t in _BUCKET_STEPS:
        large = large + (n >= t).astype(jnp.int32)
    bucket = jnp.where(n < 8, n, large) + jnp.where(rel > 0, N_BUCKETS // 2, 0)
    val = jnp.zeros(rel.shape, F32)
    for b in range(N_BUCKETS):
        val = jnp.where(bucket == b, tab_ref[b, col], val)
    return val


def _da_bias_kernel(tab_ref, o_ref):
    h = pl.program_id(0)
    d = pl.program_id(1) - 2
    row = lax.broadcasted_iota(jnp.int32, (T_ATT, T_ATT), 0)
    col = lax.broadcasted_iota(jnp.int32, (T_ATT, T_ATT), 1)
    o_ref[0, 0] = _bias_lookup(col - row + d * T_ATT, tab_ref, h)


def _da_bias_tiles(rel_bias):
    return pl.pallas_call(
        _da_bias_kernel,
        grid=(DA_HEADS, 5),
        in_specs=[pl.BlockSpec(memory_space=pltpu.SMEM)],
        out_specs=pl.BlockSpec((1, 1, T_ATT, T_ATT), lambda h, d: (h, d, 0, 0)),
        out_shape=jax.ShapeDtypeStruct((DA_HEADS, 5, T_ATT, T_ATT), F32),
        compiler_params=_cparams(("arbitrary", "arbitrary"), 32),
    )(rel_bias)


def _dil_bias_kernel(tab_ref, o_ref):
    g = pl.program_id(0)
    v = pl.program_id(1)
    h = pl.program_id(2)
    row = lax.broadcasted_iota(jnp.int32, (TQ_DIL, W_DIL), 0)
    col = lax.broadcasted_iota(jnp.int32, (TQ_DIL, W_DIL), 1)
    rel = col - row - HALF_DIL * v
    dilation = lax.shift_left(jnp.int32(1), 2 * g)
    val = _bias_lookup(rel * dilation, tab_ref, DA_HEADS + g * DIL_HEADS + h)
    o_ref[0, 0, 0] = jnp.where(jnp.abs(rel) <= HALF_DIL, val, NEG_INF)


def _dil_bias_tiles(rel_bias):
    return pl.pallas_call(
        _dil_bias_kernel,
        grid=(N_DIL, 3, DIL_HEADS),
        in_specs=[pl.BlockSpec(memory_space=pltpu.SMEM)],
        out_specs=pl.BlockSpec((1, 1, 1, TQ_DIL, W_DIL), lambda g, v, h: (g, v, h, 0, 0)),
        out_shape=jax.ShapeDtypeStruct((N_DIL, 3, DIL_HEADS, TQ_DIL, W_DIL), F32),
        compiler_params=_cparams(("arbitrary",) * 3, 32),
    )(rel_bias)


_QK_BLOCKS = {0: 0, 1: 1, 2: 2, 3: 3}
for _g in range(N_DIL):
    _QK_BLOCKS[6 + 3 * _g] = 4 + 2 * _g
    _QK_BLOCKS[7 + 3 * _g] = 5 + 2 * _g


def _inproj_kernel(x_ref, mod_ref, g1_ref, w_ref, gain_ref, gmat_ref,
                   qa_ref, ka_ref, va_ref, dil_ref, gate_ref):
    x = x_ref[0]
    mod = mod_ref[0]
    ms = jnp.mean(x * x, axis=-1, keepdims=True)
    h = (x * lax.rsqrt(ms + EPS)) * g1_ref[...]
    h = h * (1.0 + mod[1:2]) + mod[0:1]
    hb = h.astype(BF16)
    gmat = gmat_ref[...]

    for blk in range(N_COL_BLKS):
        acc = jnp.dot(hb, w_ref[:, blk * COL_BLK:(blk + 1) * COL_BLK], preferred_element_type=F32)
        if blk < 6:
            dst, off = (qa_ref, ka_ref, va_ref)[blk // 2], (blk % 2) * COL_BLK
        elif blk < 15:
            dst, off = dil_ref, (blk - 6) * COL_BLK
        else:
            dst, off = gate_ref, (blk - 15) * COL_BLK
        if blk in _QK_BLOCKS:
            gi = _QK_BLOCKS[blk]
            for half in range(COL_BLK // NORM_W):
                lo = half * NORM_W
                a = acc[:, lo:lo + NORM_W]
                hm = jnp.dot((a * a).astype(BF16), gmat, preferred_element_type=F32)
                y = (a * lax.rsqrt(hm + EPS)) * gain_ref[gi:gi + 1, lo:lo + NORM_W]
                dst[0, :, off + lo:off + lo + NORM_W] = y.astype(BF16)
        elif blk >= 15:
            dst[0, :, off:off + COL_BLK] = (1.0 / (1.0 + jnp.exp(-acc))).astype(BF16)
        else:
            dst[0, :, off:off + COL_BLK] = acc.astype(BF16)


def _inproj(x, mod, g1, w_in_b, gains, gmat):
    B, S, _ = x.shape
    tm = TM_PROJ
    const = lambda b, i: (0, 0)
    tok = lambda b, i: (b, i, 0)
    out_shapes = (jax.ShapeDtypeStruct((B, S, 2 * DA_QK), BF16),
                  jax.ShapeDtypeStruct((B, S, 2 * DA_QK), BF16),
                  jax.ShapeDtypeStruct((B, S, DA_V), BF16),
                  jax.ShapeDtypeStruct((B, S, DIL_COLS), BF16),
                  jax.ShapeDtypeStruct((B, S, 2 * D_MODEL), BF16))
    return pl.pallas_call(
        _inproj_kernel,
        grid=(B, S // tm),
        in_specs=[pl.BlockSpec((1, tm, D_MODEL), tok),
                  pl.BlockSpec((1, 6, D_MODEL), lambda b, i: (b, 0, 0)),
                  pl.BlockSpec((1, D_MODEL), const),
                  pl.BlockSpec((D_MODEL, IN_COLS), const, pipeline_mode=pl.Buffered(1)),
                  pl.BlockSpec(gains.shape, const),
                  pl.BlockSpec((NORM_W, NORM_W), const)],
        out_specs=tuple(pl.BlockSpec((1, tm, s.shape[-1]), tok) for s in out_shapes),
        out_shape=out_shapes,
        compiler_params=_cparams(("arbitrary", "arbitrary"), 56),
    )(x, mod, g1, w_in_b, gains, gmat)


def _diffattn_kernel(lam_ref, q_ref, k_ref, v_ref, bias_ref, g_ref, o_ref):
    T = T_ATT
    i = pl.program_id(2)
    n_kv = k_ref.shape[1] // T

    q = q_ref[0]
    lane = lax.broadcasted_iota(jnp.int32, q.shape, 1)
    zero = jnp.zeros_like(q)
    qs = (jnp.where(lane < HEAD_DIM, q, zero), jnp.where(lane >= HEAD_DIM, q, zero))

    def body(j, carry):
        kb = k_ref[0, pl.ds(pl.multiple_of(j * T, T), T), :]
        vb = v_ref[0, pl.ds(pl.multiple_of(j * T, T), T), :]
        bt = bias_ref[0, jnp.clip(j - i, -2, 2) + 2]
        new = []
        for a in range(2):
            m, l, acc = carry[a]
            s = lax.dot_general(qs[a], kb, (((1,), (1,)), ((), ())), preferred_element_type=F32) + bt
            m_new = jnp.maximum(m, jnp.max(s, axis=-1, keepdims=True))
            alpha = jnp.exp(m - m_new)
            p = jnp.exp(s - m_new)
            l = alpha * l + jnp.sum(p, axis=-1, keepdims=True)
            acc = alpha * acc + jnp.dot(p.astype(BF16), vb, preferred_element_type=F32)
            new.append((m_new, l, acc))
        return tuple(new)

    init = tuple((jnp.full((T, 1), NEG_INF, F32), jnp.zeros((T, 1), F32), jnp.zeros((T, DA_VDIM), F32))
                 for _ in range(2))
    (_, l1, a1), (_, l2, a2) = lax.fori_loop(0, n_kv, body, init)

    lp = lam_ref[...]
    lam = (jnp.exp(jnp.sum(lp[0:1] * lp[1:2], axis=-1, keepdims=True))
           - jnp.exp(jnp.sum(lp[2:3] * lp[3:4], axis=-1, keepdims=True)) + LAM_INIT)
    o = a1 / l1 - lam * (a2 / l2)
    ms = jnp.mean(o * o, axis=-1, keepdims=True)
    o = (o * lax.rsqrt(ms + EPS)) * g_ref[...] * (1.0 - LAM_INIT)
    o_ref[0] = o.astype(o_ref.dtype)


def _diff_attention(lam_params, qa, ka, va, bias_tiles, subln_g):
    B, S, _ = qa.shape
    T = T_ATT
    return pl.pallas_call(
        _diffattn_kernel,
        grid=(B, DA_HEADS, S // T),
        in_specs=[pl.BlockSpec((4, HEAD_DIM), lambda b, h, i: (0, 0)),
                  pl.BlockSpec((1, T, LANES), lambda b, h, i: (b, i, h)),
                  pl.BlockSpec((1, S, LANES), lambda b, h, i: (b, 0, h)),
                  pl.BlockSpec((1, S, LANES), lambda b, h, i: (b, 0, h)),
                  pl.BlockSpec((1, 5, T, T), lambda b, h, i: (h, 0, 0, 0)),
                  pl.BlockSpec((1, DA_VDIM), lambda b, h, i: (0, 0))],
        out_specs=pl.BlockSpec((1, T, LANES), lambda b, h, i: (b, i, h)),
        out_shape=jax.ShapeDtypeStruct((B, S, DA_V), BF16),
        compiler_params=_cparams(("arbitrary",) * 3, 48),
    )(lam_params, qa, ka, va, bias_tiles, subln_g)


def _dilattn_kernel(q_ref, k_ref, v_ref, bias_ref, o_ref, lse_ref):
    i = pl.program_id(2)
    n_q = pl.num_programs(2)
    L = k_ref.shape[1]
    w0 = pl.multiple_of(jnp.clip(i * TQ_DIL - HALF_DIL, 0, L - W_DIL), HALF_DIL)
    variant = jnp.where(i == 0, 0, jnp.where(i == n_q - 1, 2, 1))

    lane = lax.broadcasted_iota(jnp.int32, (TQ_DIL, LANES), 1)
    low = lane < HEAD_DIM
    for hp in range(DIL_HEADS // 2):
        cs = slice(hp * LANES, (hp + 1) * LANES)
        qp = q_ref[0, :, cs]
        kp = k_ref[0, pl.ds(w0, W_DIL), cs]
        vp = v_ref[0, pl.ds(w0, W_DIL), cs]
        zero = jnp.zeros_like(qp)
        outs, lses = [], []
        for half in range(2):
            qm = jnp.where(low, qp, zero) if half == 0 else jnp.where(low, zero, qp)
            s = lax.dot_general(qm, kp, (((1,), (1,)), ((), ())), preferred_element_type=F32)
            s = s + bias_ref[0, variant, 2 * hp + half]
            m = jnp.max(s, axis=-1, keepdims=True)
            p = jnp.exp(s - m)
            l = jnp.sum(p, axis=-1, keepdims=True)
            outs.append(jnp.dot(p.astype(BF16), vp, preferred_element_type=F32) / l)
            lses.append(m + jnp.log(l))
        o_ref[0, :, cs] = jnp.where(low, outs[0], outs[1]).astype(o_ref.dtype)
        lse_ref[0, :, cs] = jnp.where(low, lses[0], lses[1])


def _dilated_attention(dil, bias_tiles, g, dilation):
    B, S, _ = dil.shape
    r = dilation
    L = S // r
    dil_v = dil.reshape(B, L, r * DIL_COLS)
    n_seg = DIL_COLS // DIL_W
    qmap = lambda b, c, i: (b, i, c * n_seg + 3 * g)
    kmap = lambda b, c, i: (b, 0, c * n_seg + 3 * g + 1)
    vmap = lambda b, c, i: (b, 0, c * n_seg + 3 * g + 2)
    omap = lambda b, c, i: (b, i, c)
    o, lse = pl.pallas_call(
        _dilattn_kernel,
        grid=(B, r, L // TQ_DIL),
        in_specs=[pl.BlockSpec((1, TQ_DIL, DIL_W), qmap),
                  pl.BlockSpec((1, L, DIL_W), kmap, pipeline_mode=pl.Buffered(1)),
                  pl.BlockSpec((1, L, DIL_W), vmap, pipeline_mode=pl.Buffered(1)),
                  pl.BlockSpec((1, 3, DIL_HEADS, TQ_DIL, W_DIL), lambda b, c, i: (g, 0, 0, 0, 0))],
        out_specs=(pl.BlockSpec((1, TQ_DIL, DIL_W), omap), pl.BlockSpec((1, TQ_DIL, DIL_W), omap)),
        out_shape=(jax.ShapeDtypeStruct((B, L, r * DIL_W), BF16),
                   jax.ShapeDtypeStruct((B, L, r * DIL_W), F32)),
        compiler_params=_cparams(("arbitrary",) * 3, 48),
    )(dil_v, dil_v, dil_v, bias_tiles)
    return o.reshape(B, S, DIL_W), lse.reshape(B, S, DIL_W)


def _merge_kernel(x_ref, mod_ref, ya_ref, o0_ref, o1_ref, o2_ref, l0_ref, l1_ref, l2_ref, gate_ref,
                  wa_ref, wb_ref, wo_ref, out_ref):
    lses = (l0_ref[0], l1_ref[0], l2_ref[0])
    mx = jnp.maximum(jnp.maximum(lses[0], lses[1]), lses[2])
    es = [jnp.exp(l - mx) for l in lses]
    den = es[0] + es[1] + es[2]
    num = (es[0] * o0_ref[0].astype(F32) + es[1] * o1_ref[0].astype(F32) + es[2] * o2_ref[0].astype(F32))
    yb = (num / den).astype(BF16)
    pa = jnp.dot(ya_ref[0], wa_ref[...], preferred_element_type=F32)
    pb = jnp.dot(yb, wb_ref[...], preferred_element_type=F32)
    gates = gate_ref[0]
    merged = gates[:, :D_MODEL].astype(F32) * pa + gates[:, D_MODEL:].astype(F32) * pb
    z = jnp.dot(merged.astype(BF16), wo_ref[...], preferred_element_type=F32)
    out_ref[0] = x_ref[0] + mod_ref[0][2:3] * z


def _merge(x, mod, ya, outs, lses, gates, wa, wb, wo):
    B, S, _ = x.shape
    tm = TM_OUT
    tok = lambda b, i: (b, i, 0)
    const = lambda b, i: (0, 0)
    tspec = lambda w: pl.BlockSpec((1, tm, w), tok)
    return pl.pallas_call(
        _merge_kernel,
        grid=(B, S // tm),
        in_specs=[tspec(D_MODEL), pl.BlockSpec((1, 6, D_MODEL), lambda b, i: (b, 0, 0)), tspec(DA_V),
                  tspec(DIL_W), tspec(DIL_W), tspec(DIL_W), tspec(DIL_W), tspec(DIL_W), tspec(DIL_W),
                  tspec(2 * D_MODEL),
                  pl.BlockSpec((DA_V, D_MODEL), const, pipeline_mode=pl.Buffered(1)),
                  pl.BlockSpec((DIL_W, D_MODEL), const, pipeline_mode=pl.Buffered(1)),
                  pl.BlockSpec((D_MODEL, D_MODEL), const, pipeline_mode=pl.Buffered(1))],
        out_specs=tspec(D_MODEL),
        out_shape=jax.ShapeDtypeStruct((B, S, D_MODEL), F32),
        compiler_params=_cparams(("arbitrary", "arbitrary"), 48),
    )(x, mod, ya, *outs, *lses, gates, wa, wb, wo)


def _mlp_kernel(x_ref, mod_ref, g2_ref, wu_ref, wd_ref, out_ref):
    x = x_ref[0]
    mod = mod_ref[0]
    ms = jnp.mean(x * x, axis=-1, keepdims=True)
    h = (x * lax.rsqrt(ms + EPS)) * g2_ref[...]
    hb = (h * (1.0 + mod[4:5]) + mod[3:4]).astype(BF16)
    acc = jnp.zeros(x.shape, F32)
    for f in range(D_FF // D_MODEL):
        cs = slice(f * D_MODEL, (f + 1) * D_MODEL)
        u = jnp.maximum(jnp.dot(hb, wu_ref[:, cs], preferred_element_type=F32), 0.0)
        acc = acc + jnp.dot((u * u).astype(BF16), wd_ref[cs, :], preferred_element_type=F32)
    out_ref[0] = x + mod[5:6] * acc


def _mlp(x, mod, g2, wu, wd):
    B, S, _ = x.shape
    tm = TM_OUT
    tok = lambda b, i: (b, i, 0)
    const = lambda b, i: (0, 0)
    return pl.pallas_call(
        _mlp_kernel,
        grid=(B, S // tm),
        in_specs=[pl.BlockSpec((1, tm, D_MODEL), tok),
                  pl.BlockSpec((1, 6, D_MODEL), lambda b, i: (b, 0, 0)),
                  pl.BlockSpec((1, D_MODEL), const),
                  pl.BlockSpec((D_MODEL, D_FF), const, pipeline_mode=pl.Buffered(1)),
                  pl.BlockSpec((D_FF, D_MODEL), const, pipeline_mode=pl.Buffered(1))],
        out_specs=pl.BlockSpec((1, tm, D_MODEL), tok),
        out_shape=jax.ShapeDtypeStruct((B, S, D_MODEL), F32),
        compiler_params=_cparams(("arbitrary", "arbitrary"), 48),
    )(x, mod, g2, wu, wd)


def _pair_heads(a, b):
    d = a.shape[0]
    return jnp.stack([a.reshape(d, DA_HEADS, HEAD_DIM), b.reshape(d, DA_HEADS, HEAD_DIM)],
                     axis=2).reshape(d, 2 * DA_QK)


def _layer(x, mod, p):
    qa, ka, va, dil, gates = _inproj(x, mod, p["g1"], p["w_in"], p["gains"], p["gmat"])
    ya = _diff_attention(p["lam"], qa, ka, va, p["da_bias"], p["subln_g"])
    outs, lses = [], []
    for g, (_, dilation) in enumerate(DIL_CONFIG):
        o, lse = _dilated_attention(dil, p["dil_bias"], g, dilation)
        outs.append(o)
        lses.append(lse)
    x1 = _merge(x, mod, ya, outs, lses, gates, p["w_br_a"], p["w_br_b"], p["w_o"])
    return _mlp(x1, mod, p["g2"], p["w_up"], p["w_down"])


def kernel(x_prompt, x_sample, c_prompt, c_sample, rel_bias, norm1_g, w_ada, b_ada, w_in, qn_a, kn_a,
           lambda_q1, lambda_k1, lambda_q2, lambda_k2, subln_g, qn_b, kn_b, w_br_a, w_br_b, w_o,
           norm2_g, w_up, w_down):
    nbp, nbs = c_prompt.shape[0], c_sample.shape[0]
    pad = (-(nbp + nbs)) % 8
    c_all = jnp.concatenate([c_prompt, c_sample, jnp.zeros((pad, D_MODEL), F32)], axis=0)
    mod = _modulation(c_all, w_ada[0], b_ada[0]).reshape(-1, 6, D_MODEL)

    w = w_in[0]
    q1, q2, k1, k2 = (w[:, n * DA_QK:(n + 1) * DA_QK] for n in range(4))
    w_perm = jnp.concatenate([_pair_heads(q1, q2), _pair_heads(k1, k2), w[:, 4 * DA_QK:]], axis=1)

    scale = HEAD_DIM ** -0.5
    tile8 = lambda v: jnp.tile(v, DIL_W // HEAD_DIM)
    gain_rows = [tile8(qn_a[0]) * scale] * 2 + [tile8(kn_a[0])] * 2
    for g in range(N_DIL):
        gain_rows += [tile8(qn_b[0, g]) * scale, tile8(kn_b[0, g])]
    ids = jnp.arange(NORM_W) // HEAD_DIM
    gmat = jnp.where(ids[:, None] == ids[None, :], 1.0 / HEAD_DIM, 0.0).astype(BF16)

    p = {
        "g1": norm1_g[0].reshape(1, D_MODEL),
        "g2": norm2_g[0].reshape(1, D_MODEL),
        "w_in": w_perm.astype(BF16),
        "gains": jnp.stack(gain_rows, axis=0),
        "gmat": gmat,
        "lam": jnp.stack([lambda_q1[0], lambda_k1[0], lambda_q2[0], lambda_k2[0]], axis=0),
        "subln_g": subln_g[0].reshape(1, DA_VDIM),
        "da_bias": _da_bias_tiles(rel_bias),
        "dil_bias": _dil_bias_tiles(rel_bias),
        "w_br_a": w_br_a[0].astype(BF16),
        "w_br_b": w_br_b[0].astype(BF16),
        "w_o": w_o[0].astype(BF16),
        "w_up": w_up[0].astype(BF16),
        "w_down": w_down[0].astype(BF16),
    }
    y_prompt = _layer(x_prompt, mod[:nbp], p)
    y_sample = _layer(x_sample, mod[nbp:nbp + nbs], p)
    return (y_prompt, y_sample)
```

```python
import functools
import math

import jax
import jax.numpy as jnp
from jax import lax
from jax.experimental import pallas as pl
from jax.experimental.pallas import tpu as pltpu

F32 = jnp.float32
BF16 = jnp.bfloat16

D_MODEL = 1024
HEAD_DIM = 64
DA_HEADS = 8
DA_VDIM = 2 * HEAD_DIM
DIL_CONFIG = ((128, 1), (512, 4), (2048, 16))
N_DIL = 3
DIL_HEADS = 8
D_FF = 4 * D_MODEL
N_BUCKETS = 32
NEG_INF = -1e30
EPS = 1e-6
LAM_INIT = 0.8 - 0.6 * math.exp(-0.3 * 0)

DA_QK = DA_HEADS * HEAD_DIM
DA_V = DA_HEADS * DA_VDIM
DIL_W = DIL_HEADS * HEAD_DIM
DIL_COLS = 3 * N_DIL * DIL_W
IN_COLS = 4 * DA_QK + DA_V + DIL_COLS + 2 * D_MODEL

LANES = 128
COL_BLK = 512
N_COL_BLKS = IN_COLS // COL_BLK
NORM_W = 256

TM_PROJ = 256
TM_OUT = 256
T_ATT = 512
TQ_DIL = 128
HALF_DIL = 64
W_DIL = TQ_DIL + 2 * HALF_DIL

LOG2E = math.log2(math.e)
DA_HEADROOM = 64.0
DA_SUM_MIN = 2.0 ** -60
DA_SUM_MAX = 2.0 ** 100
DA_UNROLL = 4
BF16_ROUND_MARGIN = 1.02

_BUCKET_STEPS = (12, 16, 23, 32, 46, 64, 91)


def _cparams(sem, vmem_mb):
    return pltpu.CompilerParams(dimension_semantics=sem, vmem_limit_bytes=vmem_mb * 1024 * 1024)


def _mod_kernel(c_ref, w_ref, b_ref, o_ref):
    c = c_ref[...]
    a = c * (1.0 / (1.0 + jnp.exp(-c)))
    o_ref[...] = jnp.dot(a, w_ref[...], preferred_element_type=F32,
                         precision=lax.Precision.HIGHEST) + b_ref[...]


def _modulation(c_all, w_ada, b_ada):
    rows = c_all.shape[0]
    n = w_ada.shape[1]
    return pl.pallas_call(
        _mod_kernel,
        grid=(n // D_MODEL,),
        in_specs=[pl.BlockSpec((rows, D_MODEL), lambda j: (0, 0)),
                  pl.BlockSpec((D_MODEL, D_MODEL), lambda j: (0, j)),
                  pl.BlockSpec((1, D_MODEL), lambda j: (0, j))],
        out_specs=pl.BlockSpec((rows, D_MODEL), lambda j: (0, j)),
        out_shape=jax.ShapeDtypeStruct((rows, n), F32),
        compiler_params=_cparams(("arbitrary",), 32),
        name="modulation",
    )(c_all, w_ada, b_ada.reshape(1, n))


def _bias_lookup(rel, tab_ref, col):
    n = jnp.abs(rel)
    large = jnp.full(rel.shape, 8, jnp.int32)
    for t in _BUCKET_STEPS:
        large = large + (n >= t).astype(jnp.int32)
    bucket = jnp.where(n < 8, n, large) + jnp.where(rel > 0, N_BUCKETS // 2, 0)
    val = jnp.zeros(rel.shape, F32)
    for b in range(N_BUCKETS):
        val = jnp.where(bucket == b, tab_ref[b, col], val)
    return val


def _da_bias_kernel(tab_ref, dac_ref, o_ref):
    h = pl.program_id(0)
    d = pl.program_id(1) - 2

    @pl.when(jnp.abs(d) <= 1)
    def _():
        row = lax.broadcasted_iota(jnp.int32, (T_ATT, T_ATT), 0)
        col = lax.broadcasted_iota(jnp.int32, (T_ATT, T_ATT), 1)
        o_ref[0, 0] = LOG2E * _bias_lookup(col - row + d * T_ATT, tab_ref, h) - dac_ref[h, 0]

    @pl.when(jnp.abs(d) == 2)
    def _():
        o_ref[0, 0] = jnp.full((T_ATT, T_ATT), -dac_ref[h, jnp.where(d < 0, 1, 2)], F32)


def _da_bias_tiles(rel_bias, dac):
    assert T_ATT >= _BUCKET_STEPS[-1]
    return pl.pallas_call(
        _da_bias_kernel,
        grid=(DA_HEADS, 5),
        in_specs=[pl.BlockSpec(memory_space=pltpu.SMEM), pl.BlockSpec(memory_space=pltpu.SMEM)],
        out_specs=pl.BlockSpec((1, 1, T_ATT, T_ATT), lambda h, d: (h, d, 0, 0)),
        out_shape=jax.ShapeDtypeStruct((DA_HEADS, 5, T_ATT, T_ATT), F32),
        compiler_params=_cparams(("arbitrary", "arbitrary"), 32),
        name="da_bias_tiles",
    )(rel_bias, dac)


def _dil_bias_kernel(tab_ref, o_ref):
    g = pl.program_id(0)
    v = pl.program_id(1)
    h = pl.program_id(2)
    row = lax.broadcasted_iota(jnp.int32, (TQ_DIL, W_DIL), 0)
    col = lax.broadcasted_iota(jnp.int32, (TQ_DIL, W_DIL), 1)
    rel = col - row - HALF_DIL * v
    dilation = lax.shift_left(jnp.int32(1), 2 * g)
    val = _bias_lookup(rel * dilation, tab_ref, DA_HEADS + g * DIL_HEADS + h)
    o_ref[0, 0, 0] = jnp.where(jnp.abs(rel) <= HALF_DIL, val, NEG_INF)


def _dil_bias_tiles(rel_bias):
    return pl.pallas_call(
        _dil_bias_kernel,
        grid=(N_DIL, 3, DIL_HEADS),
        in_specs=[pl.BlockSpec(memory_space=pltpu.SMEM)],
        out_specs=pl.BlockSpec((1, 1, 1, TQ_DIL, W_DIL), lambda g, v, h: (g, v, h, 0, 0)),
        out_shape=jax.ShapeDtypeStruct((N_DIL, 3, DIL_HEADS, TQ_DIL, W_DIL), F32),
        compiler_params=_cparams(("arbitrary",) * 3, 32),
        name="dil_bias_tiles",
    )(rel_bias)


_QK_BLOCKS = {0: 0, 1: 1, 2: 2, 3: 3}
for _g in range(N_DIL):
    _QK_BLOCKS[6 + 3 * _g] = 4 + 2 * _g
    _QK_BLOCKS[7 + 3 * _g] = 5 + 2 * _g


def _inproj_kernel(x_ref, mod_ref, g1_ref, w_ref, gain_ref, gmat_ref,
                   qa_ref, ka_ref, va_ref, dil_ref, gate_ref):
    x = x_ref[0]
    mod = mod_ref[0]
    ms = jnp.mean(x * x, axis=-1, keepdims=True)
    h = (x * lax.rsqrt(ms + EPS)) * g1_ref[...]
    h = h * (1.0 + mod[1:2]) + mod[0:1]
    hb = h.astype(BF16)
    gmat = gmat_ref[...]

    for blk in range(N_COL_BLKS):
        acc = jnp.dot(hb, w_ref[:, blk * COL_BLK:(blk + 1) * COL_BLK], preferred_element_type=F32)
        if blk < 6:
            dst, off = (qa_ref, ka_ref, va_ref)[blk // 2], (blk % 2) * COL_BLK
        elif blk < 15:
            dst, off = dil_ref, (blk - 6) * COL_BLK
        else:
            dst, off = gate_ref, (blk - 15) * COL_BLK
        if blk in _QK_BLOCKS:
            gi = _QK_BLOCKS[blk]
            for half in range(COL_BLK // NORM_W):
                lo = half * NORM_W
                a = acc[:, lo:lo + NORM_W]
                hm = jnp.dot((a * a).astype(BF16), gmat, preferred_element_type=F32)
                y = (a * lax.rsqrt(hm + EPS)) * gain_ref[gi:gi + 1, lo:lo + NORM_W]
                dst[0, :, off + lo:off + lo + NORM_W] = y.astype(BF16)
        elif blk >= 15:
            dst[0, :, off:off + COL_BLK] = (1.0 / (1.0 + jnp.exp(-acc))).astype(BF16)
        else:
            dst[0, :, off:off + COL_BLK] = acc.astype(BF16)


def _inproj(x, mod, g1, w_in_b, gains, gmat):
    B, S, _ = x.shape
    tm = TM_PROJ
    const = lambda b, i: (0, 0)
    tok = lambda b, i: (b, i, 0)
    out_shapes = (jax.ShapeDtypeStruct((B, S, 2 * DA_QK), BF16),
                  jax.ShapeDtypeStruct((B, S, 2 * DA_QK), BF16),
                  jax.ShapeDtypeStruct((B, S, DA_V), BF16),
                  jax.ShapeDtypeStruct((B, S, DIL_COLS), BF16),
                  jax.ShapeDtypeStruct((B, S, 2 * D_MODEL), BF16))
    return pl.pallas_call(
        _inproj_kernel,
        grid=(B, S // tm),
        in_specs=[pl.BlockSpec((1, tm, D_MODEL), tok),
                  pl.BlockSpec((1, 6, D_MODEL), lambda b, i: (b, 0, 0)),
                  pl.BlockSpec((1, D_MODEL), const),
                  pl.BlockSpec((D_MODEL, IN_COLS), const, pipeline_mode=pl.Buffered(1)),
                  pl.BlockSpec(gains.shape, const),
                  pl.BlockSpec((NORM_W, NORM_W), const)],
        out_specs=tuple(pl.BlockSpec((1, tm, s.shape[-1]), tok) for s in out_shapes),
        out_shape=out_shapes,
        compiler_params=_cparams(("arbitrary", "arbitrary"), 56),
        name="in_projection",
    )(x, mod, g1, w_in_b, gains, gmat)


def _diffattn_kernel(lam_ref, q_ref, k_ref, v_ref, bias_ref, g_ref, o_ref, vext_ref, acc_ref):
    T = T_ATT
    i = pl.program_id(2)
    n_kv = k_ref.shape[1] // T

    @pl.when(i == 0)
    def _():
        vext_ref[:, :DA_VDIM] = v_ref[0]
        vext_ref[:, DA_VDIM:] = jnp.ones((vext_ref.shape[0], DA_VDIM), BF16)

    q = q_ref[0]
    lane = lax.broadcasted_iota(jnp.int32, q.shape, 1)
    zero = jnp.zeros_like(q)
    qs = jnp.concatenate([jnp.where(lane < HEAD_DIM, q, zero), jnp.where(lane >= HEAD_DIM, q, zero)], axis=0)

    def kv_rows(j):
        return pl.ds(pl.multiple_of(j * T, T), T)

    def logits(j):
        s = lax.dot_general(qs, k_ref[0, kv_rows(j), :], (((1,), (1,)), ((), ())),
                            preferred_element_type=F32)
        return s + jnp.tile(bias_ref[0, jnp.clip(j - i, -2, 2) + 2], (2, 1))

    acc_ref[...] = jnp.zeros(acc_ref.shape, F32)

    def fast_body(j, c):
        acc_ref[...] += jnp.dot(jnp.exp2(logits(j)).astype(BF16), vext_ref[kv_rows(j), :],
                                preferred_element_type=F32)
        return c

    lax.fori_loop(0, n_kv, fast_body, 0, unroll=DA_UNROLL)

    sums = acc_ref[:, DA_VDIM:]
    in_range = jnp.logical_and(jnp.min(sums) >= DA_SUM_MIN, jnp.max(sums) <= DA_SUM_MAX)

    @pl.when(jnp.logical_not(in_range))
    def _():
        def body(j, carry):
            m, acc = carry
            u = logits(j)
            m_new = jnp.maximum(m, jnp.max(u, axis=-1, keepdims=True))
            p = jnp.exp2(u - m_new)
            acc = jnp.exp2(m - m_new) * acc + jnp.dot(p.astype(BF16), vext_ref[kv_rows(j), :],
                                                       preferred_element_type=F32)
            return m_new, acc

        init = (jnp.full((2 * T, 1), NEG_INF, F32), jnp.zeros(acc_ref.shape, F32))
        acc_ref[...] = lax.fori_loop(0, n_kv, body, init)[1]

    lp = lam_ref[...]
    lam = (jnp.exp(jnp.sum(lp[0:1] * lp[1:2], axis=-1, keepdims=True))
           - jnp.exp(jnp.sum(lp[2:3] * lp[3:4], axis=-1, keepdims=True)) + LAM_INIT)
    o = (acc_ref[:T, :DA_VDIM] / acc_ref[:T, DA_VDIM:]
         - lam * (acc_ref[T:, :DA_VDIM] / acc_ref[T:, DA_VDIM:]))
    ms = jnp.mean(o * o, axis=-1, keepdims=True)
    o = (o * lax.rsqrt(ms + EPS)) * g_ref[...] * (1.0 - LAM_INIT)
    o_ref[0] = o.astype(o_ref.dtype)


def _diff_attention(lam_params, qa, ka, va, bias_tiles, subln_g):
    B, S, _ = qa.shape
    T = T_ATT
    return pl.pallas_call(
        _diffattn_kernel,
        grid=(B, DA_HEADS, S // T),
        in_specs=[pl.BlockSpec((4, HEAD_DIM), lambda b, h, i: (0, 0)),
                  pl.BlockSpec((1, T, LANES), lambda b, h, i: (b, i, h)),
                  pl.BlockSpec((1, S, LANES), lambda b, h, i: (b, 0, h)),
                  pl.BlockSpec((1, S, LANES), lambda b, h, i: (b, 0, h)),
                  pl.BlockSpec((1, 5, T, T), lambda b, h, i: (h, 0, 0, 0)),
                  pl.BlockSpec((1, DA_VDIM), lambda b, h, i: (0, 0))],
        out_specs=pl.BlockSpec((1, T, LANES), lambda b, h, i: (b, i, h)),
        out_shape=jax.ShapeDtypeStruct((B, S, DA_V), BF16),
        scratch_shapes=[pltpu.VMEM((S, 2 * DA_VDIM), BF16), pltpu.VMEM((2 * T, 2 * DA_VDIM), F32)],
        compiler_params=_cparams(("arbitrary",) * 3, 48),
        name="diff_attention",
    )(lam_params, qa, ka, va, bias_tiles, subln_g)


def _dilattn_kernel(q_ref, k_ref, v_ref, bias_ref, o_ref, lse_ref):
    i = pl.program_id(2)
    n_q = pl.num_programs(2)
    L = k_ref.shape[1]
    w0 = pl.multiple_of(jnp.clip(i * TQ_DIL - HALF_DIL, 0, L - W_DIL), HALF_DIL)
    variant = jnp.where(i == 0, 0, jnp.where(i == n_q - 1, 2, 1))

    lane = lax.broadcasted_iota(jnp.int32, (TQ_DIL, LANES), 1)
    low = lane < HEAD_DIM
    for hp in range(DIL_HEADS // 2):
        cs = slice(hp * LANES, (hp + 1) * LANES)
        qp = q_ref[0, :, cs]
        kp = k_ref[0, pl.ds(w0, W_DIL), cs]
        vp = v_ref[0, pl.ds(w0, W_DIL), cs]
        zero = jnp.zeros_like(qp)
        outs, lses = [], []
        for half in range(2):
            qm = jnp.where(low, qp, zero) if half == 0 else jnp.where(low, zero, qp)
            s = lax.dot_general(qm, kp, (((1,), (1,)), ((), ())), preferred_element_type=F32)
            s = s + bias_ref[0, variant, 2 * hp + half]
            m = jnp.max(s, axis=-1, keepdims=True)
            p = jnp.exp(s - m)
            l = jnp.sum(p, axis=-1, keepdims=True)
            outs.append(jnp.dot(p.astype(BF16), vp, preferred_element_type=F32) / l)
            lses.append(m + jnp.log(l))
        o_ref[0, :, cs] = jnp.where(low, outs[0], outs[1]).astype(o_ref.dtype)
        lse_ref[0, :, cs] = jnp.where(low, lses[0], lses[1])


def _dilated_attention(dil, bias_tiles, g, dilation):
    B, S, _ = dil.shape
    r = dilation
    L = S // r
    dil_v = dil.reshape(B, L, r * DIL_COLS)
    n_seg = DIL_COLS // DIL_W
    qmap = lambda b, c, i: (b, i, c * n_seg + 3 * g)
    kmap = lambda b, c, i: (b, 0, c * n_seg + 3 * g + 1)
    vmap = lambda b, c, i: (b, 0, c * n_seg + 3 * g + 2)
    omap = lambda b, c, i: (b, i, c)
    o, lse = pl.pallas_call(
        _dilattn_kernel,
        grid=(B, r, L // TQ_DIL),
        in_specs=[pl.BlockSpec((1, TQ_DIL, DIL_W), qmap),
                  pl.BlockSpec((1, L, DIL_W), kmap, pipeline_mode=pl.Buffered(1)),
                  pl.BlockSpec((1, L, DIL_W), vmap, pipeline_mode=pl.Buffered(1)),
                  pl.BlockSpec((1, 3, DIL_HEADS, TQ_DIL, W_DIL), lambda b, c, i: (g, 0, 0, 0, 0))],
        out_specs=(pl.BlockSpec((1, TQ_DIL, DIL_W), omap), pl.BlockSpec((1, TQ_DIL, DIL_W), omap)),
        out_shape=(jax.ShapeDtypeStruct((B, L, r * DIL_W), BF16),
                   jax.ShapeDtypeStruct((B, L, r * DIL_W), F32)),
        compiler_params=_cparams(("arbitrary",) * 3, 48),
        name=f"dilated_attention_r{r}",
    )(dil_v, dil_v, dil_v, bias_tiles)
    return o.reshape(B, S, DIL_W), lse.reshape(B, S, DIL_W)


def _merge_kernel(x_ref, mod_ref, ya_ref, o0_ref, o1_ref, o2_ref, l0_ref, l1_ref, l2_ref, gate_ref,
                  wa_ref, wb_ref, wo_ref, out_ref):
    lses = (l0_ref[0], l1_ref[0], l2_ref[0])
    mx = jnp.maximum(jnp.maximum(lses[0], lses[1]), lses[2])
    es = [jnp.exp(l - mx) for l in lses]
    den = es[0] + es[1] + es[2]
    num = (es[0] * o0_ref[0].astype(F32) + es[1] * o1_ref[0].astype(F32) + es[2] * o2_ref[0].astype(F32))
    yb = (num / den).astype(BF16)
    pa = jnp.dot(ya_ref[0], wa_ref[...], preferred_element_type=F32)
    pb = jnp.dot(yb, wb_ref[...], preferred_element_type=F32)
    gates = gate_ref[0]
    merged = gates[:, :D_MODEL].astype(F32) * pa + gates[:, D_MODEL:].astype(F32) * pb
    z = jnp.dot(merged.astype(BF16), wo_ref[...], preferred_element_type=F32)
    out_ref[0] = x_ref[0] + mod_ref[0][2:3] * z


def _merge(x, mod, ya, outs, lses, gates, wa, wb, wo):
    B, S, _ = x.shape
    tm = TM_OUT
    tok = lambda b, i: (b, i, 0)
    const = lambda b, i: (0, 0)
    tspec = lambda w: pl.BlockSpec((1, tm, w), tok)
    return pl.pallas_call(
        _merge_kernel,
        grid=(B, S // tm),
        in_specs=[tspec(D_MODEL), pl.BlockSpec((1, 6, D_MODEL), lambda b, i: (b, 0, 0)), tspec(DA_V),
                  tspec(DIL_W), tspec(DIL_W), tspec(DIL_W), tspec(DIL_W), tspec(DIL_W), tspec(DIL_W),
                  tspec(2 * D_MODEL),
                  pl.BlockSpec((DA_V, D_MODEL), const, pipeline_mode=pl.Buffered(1)),
                  pl.BlockSpec((DIL_W, D_MODEL), const, pipeline_mode=pl.Buffered(1)),
                  pl.BlockSpec((D_MODEL, D_MODEL), const, pipeline_mode=pl.Buffered(1))],
        out_specs=tspec(D_MODEL),
        out_shape=jax.ShapeDtypeStruct((B, S, D_MODEL), F32),
        compiler_params=_cparams(("arbitrary", "arbitrary"), 48),
        name="merge_out_projection",
    )(x, mod, ya, *outs, *lses, gates, wa, wb, wo)


def _mlp_kernel(x_ref, mod_ref, g2_ref, wu_ref, wd_ref, out_ref):
    x = x_ref[0]
    mod = mod_ref[0]
    ms = jnp.mean(x * x, axis=-1, keepdims=True)
    h = (x * lax.rsqrt(ms + EPS)) * g2_ref[...]
    hb = (h * (1.0 + mod[4:5]) + mod[3:4]).astype(BF16)
    acc = jnp.zeros(x.shape, F32)
    for f in range(D_FF // D_MODEL):
        cs = slice(f * D_MODEL, (f + 1) * D_MODEL)
        u = jnp.maximum(jnp.dot(hb, wu_ref[:, cs], preferred_element_type=F32), 0.0)
        acc = acc + jnp.dot((u * u).astype(BF16), wd_ref[cs, :], preferred_element_type=F32)
    out_ref[0] = x + mod[5:6] * acc


def _mlp(x, mod, g2, wu, wd):
    B, S, _ = x.shape
    tm = TM_OUT
    tok = lambda b, i: (b, i, 0)
    const = lambda b, i: (0, 0)
    return pl.pallas_call(
        _mlp_kernel,
        grid=(B, S // tm),
        in_specs=[pl.BlockSpec((1, tm, D_MODEL), tok),
                  pl.BlockSpec((1, 6, D_MODEL), lambda b, i: (b, 0, 0)),
                  pl.BlockSpec((1, D_MODEL), const),
                  pl.BlockSpec((D_MODEL, D_FF), const, pipeline_mode=pl.Buffered(1)),
                  pl.BlockSpec((D_FF, D_MODEL), const, pipeline_mode=pl.Buffered(1))],
        out_specs=pl.BlockSpec((1, tm, D_MODEL), tok),
        out_shape=jax.ShapeDtypeStruct((B, S, D_MODEL), F32),
        compiler_params=_cparams(("arbitrary", "arbitrary"), 48),
        name="mlp",
    )(x, mod, g2, wu, wd)


def _pair_heads(a, b):
    d = a.shape[0]
    return jnp.stack([a.reshape(d, DA_HEADS, HEAD_DIM), b.reshape(d, DA_HEADS, HEAD_DIM)],
                     axis=2).reshape(d, 2 * DA_QK)


def _layer(x, mod, p):
    qa, ka, va, dil, gates = _inproj(x, mod, p["g1"], p["w_in"], p["gains"], p["gmat"])
    ya = _diff_attention(p["lam"], qa, ka, va, p["da_bias"], p["subln_g"])
    outs, lses = [], []
    for g, (_, dilation) in enumerate(DIL_CONFIG):
        o, lse = _dilated_attention(dil, p["dil_bias"], g, dilation)
        outs.append(o)
        lses.append(lse)
    x1 = _merge(x, mod, ya, outs, lses, gates, p["w_br_a"], p["w_br_b"], p["w_o"])
    return _mlp(x1, mod, p["g2"], p["w_up"], p["w_down"])


def kernel(x_prompt, x_sample, c_prompt, c_sample, rel_bias, norm1_g, w_ada, b_ada, w_in, qn_a, kn_a,
           lambda_q1, lambda_k1, lambda_q2, lambda_k2, subln_g, qn_b, kn_b, w_br_a, w_br_b, w_o,
           norm2_g, w_up, w_down):
    nbp, nbs = c_prompt.shape[0], c_sample.shape[0]
    pad = (-(nbp + nbs)) % 8
    c_all = jnp.concatenate([c_prompt, c_sample, jnp.zeros((pad, D_MODEL), F32)], axis=0)
    mod = _modulation(c_all, w_ada[0], b_ada[0]).reshape(-1, 6, D_MODEL)

    w = w_in[0]
    q1, q2, k1, k2 = (w[:, n * DA_QK:(n + 1) * DA_QK] for n in range(4))
    w_perm = jnp.concatenate([_pair_heads(q1, q2), _pair_heads(k1, k2), w[:, 4 * DA_QK:]], axis=1)

    scale = HEAD_DIM ** -0.5
    tile8 = lambda v: jnp.tile(v, DIL_W // HEAD_DIM)
    gain_rows = [tile8(qn_a[0]) * (scale * LOG2E)] * 2 + [tile8(kn_a[0])] * 2
    for g in range(N_DIL):
        gain_rows += [tile8(qn_b[0, g]) * scale, tile8(kn_b[0, g])]
    ids = jnp.arange(NORM_W) // HEAD_DIM
    gmat = jnp.where(ids[:, None] == ids[None, :], 1.0 / HEAD_DIM, 0.0).astype(BF16)

    da_tab = rel_bias[:, :DA_HEADS]
    qk_bound = HEAD_DIM * scale * jnp.max(jnp.abs(qn_a[0])) * jnp.max(jnp.abs(kn_a[0])) * BF16_ROUND_MARGIN
    c_ref = LOG2E * (qk_bound + jnp.max(da_tab, axis=0)) - DA_HEADROOM
    dac = jnp.stack([c_ref, c_ref - LOG2E * da_tab[N_BUCKETS // 2 - 1], c_ref - LOG2E * da_tab[N_BUCKETS - 1],
                     jnp.zeros_like(c_ref)], axis=1)

    p = {
        "g1": norm1_g[0].reshape(1, D_MODEL),
        "g2": norm2_g[0].reshape(1, D_MODEL),
        "w_in": w_perm.astype(BF16),
        "gains": jnp.stack(gain_rows, axis=0),
        "gmat": gmat,
        "lam": jnp.stack([lambda_q1[0], lambda_k1[0], lambda_q2[0], lambda_k2[0]], axis=0),
        "subln_g": subln_g[0].reshape(1, DA_VDIM),
        "dac": dac,
        "da_bias": _da_bias_tiles(rel_bias, dac),
        "dil_bias": _dil_bias_tiles(rel_bias),
        "w_br_a": w_br_a[0].astype(BF16),
        "w_br_b": w_br_b[0].astype(BF16),
        "w_o": w_o[0].astype(BF16),
        "w_up": w_up[0].astype(BF16),
        "w_down": w_down[0].astype(BF16),
    }
    y_prompt = _layer(x_prompt, mod[:nbp], p)
    y_sample = _layer(x_sample, mod[nbp:nbp + nbs], p)
    return (y_prompt, y_sample)
```

```python
import functools
import math

import jax
import jax.numpy as jnp
from jax import lax
from jax.experimental import pallas as pl
from jax.experimental.pallas import tpu as pltpu

F32 = jnp.float32
BF16 = jnp.bfloat16

D_MODEL = 1024
HEAD_DIM = 64
DA_HEADS = 8
DA_VDIM = 2 * HEAD_DIM
DIL_CONFIG = ((128, 1), (512, 4), (2048, 16))
N_DIL = 3
DIL_HEADS = 8
D_FF = 4 * D_MODEL
N_BUCKETS = 32
NEG_INF = -1e30
EPS = 1e-6
LAM_INIT = 0.8 - 0.6 * math.exp(-0.3 * 0)

DA_QK = DA_HEADS * HEAD_DIM
DA_V = DA_HEADS * DA_VDIM
DIL_W = DIL_HEADS * HEAD_DIM
DIL_COLS = 3 * N_DIL * DIL_W
IN_COLS = 4 * DA_QK + DA_V + DIL_COLS + 2 * D_MODEL

LANES = 128
COL_BLK = 512
N_COL_BLKS = IN_COLS // COL_BLK
NORM_W = 256

TM_PROJ = 256
TM_OUT = 256
T_ATT = 512
TQ_ATT = 1024
TQ_DIL = 128
HALF_DIL = 64
W_DIL = TQ_DIL + 2 * HALF_DIL

LOG2E = math.log2(math.e)
DA_HEADROOM = 64.0
DA_MIN_EXP = -60.0
DA_UNROLL = 4
BF16_ROUND_MARGIN = 1.02

_BUCKET_STEPS = (12, 16, 23, 32, 46, 64, 91)


def _cparams(sem, vmem_mb):
    return pltpu.CompilerParams(dimension_semantics=sem, vmem_limit_bytes=vmem_mb * 1024 * 1024)


def _mod_kernel(c_ref, w_ref, b_ref, o_ref):
    c = c_ref[...]
    a = c * (1.0 / (1.0 + jnp.exp(-c)))
    o_ref[...] = jnp.dot(a, w_ref[...], preferred_element_type=F32,
                         precision=lax.Precision.HIGHEST) + b_ref[...]


def _modulation(c_all, w_ada, b_ada):
    rows = c_all.shape[0]
    n = w_ada.shape[1]
    return pl.pallas_call(
        _mod_kernel,
        grid=(n // D_MODEL,),
        in_specs=[pl.BlockSpec((rows, D_MODEL), lambda j: (0, 0)),
                  pl.BlockSpec((D_MODEL, D_MODEL), lambda j: (0, j)),
                  pl.BlockSpec((1, D_MODEL), lambda j: (0, j))],
        out_specs=pl.BlockSpec((rows, D_MODEL), lambda j: (0, j)),
        out_shape=jax.ShapeDtypeStruct((rows, n), F32),
        compiler_params=_cparams(("arbitrary",), 32),
        name="modulation",
    )(c_all, w_ada, b_ada.reshape(1, n))


def _bias_lookup(rel, tab_ref, col):
    n = jnp.abs(rel)
    large = jnp.full(rel.shape, 8, jnp.int32)
    for t in _BUCKET_STEPS:
        large = large + (n >= t).astype(jnp.int32)
    bucket = jnp.where(n < 8, n, large) + jnp.where(rel > 0, N_BUCKETS // 2, 0)
    val = jnp.zeros(rel.shape, F32)
    for b in range(N_BUCKETS):
        val = jnp.where(bucket == b, tab_ref[b, col], val)
    return val


def _da_bias_kernel(tab_ref, dac_ref, o_ref):
    h = pl.program_id(0)
    sat = _BUCKET_STEPS[-1]
    row = lax.broadcasted_iota(jnp.int32, (LANES, LANES), 0)
    col = lax.broadcasted_iota(jnp.int32, (LANES, LANES), 1)
    far = (jnp.full((LANES, LANES), -dac_ref[h, 1], F32), jnp.full((LANES, LANES), -dac_ref[h, 2], F32))
    for d in range(-2, 3):
        for br in range(T_ATT // LANES):
            for bc in range(T_ATT // LANES):
                base = d * T_ATT + (bc - br) * LANES
                if base + LANES - 1 <= -sat:
                    val = far[0]
                elif base - (LANES - 1) >= sat:
                    val = far[1]
                else:
                    val = LOG2E * _bias_lookup(col - row + base, tab_ref, h) - dac_ref[h, 0]
                o_ref[0, d + 2, br * LANES:(br + 1) * LANES, bc * LANES:(bc + 1) * LANES] = val


def _da_bias_tiles(rel_bias, dac):
    assert T_ATT >= _BUCKET_STEPS[-1]
    return pl.pallas_call(
        _da_bias_kernel,
        grid=(DA_HEADS,),
        in_specs=[pl.BlockSpec(memory_space=pltpu.SMEM), pl.BlockSpec(memory_space=pltpu.SMEM)],
        out_specs=pl.BlockSpec((1, 5, T_ATT, T_ATT), lambda h: (h, 0, 0, 0)),
        out_shape=jax.ShapeDtypeStruct((DA_HEADS, 5, T_ATT, T_ATT), F32),
        compiler_params=_cparams(("arbitrary",), 32),
        name="da_bias_tiles",
    )(rel_bias, dac)


def _dil_bias_kernel(tab_ref, o_ref):
    g = pl.program_id(0)
    v = pl.program_id(1)
    h = pl.program_id(2)
    row = lax.broadcasted_iota(jnp.int32, (TQ_DIL, W_DIL), 0)
    col = lax.broadcasted_iota(jnp.int32, (TQ_DIL, W_DIL), 1)
    rel = col - row - HALF_DIL * v
    dilation = lax.shift_left(jnp.int32(1), 2 * g)
    val = _bias_lookup(rel * dilation, tab_ref, DA_HEADS + g * DIL_HEADS + h)
    o_ref[0, 0, 0] = jnp.where(jnp.abs(rel) <= HALF_DIL, val, NEG_INF)


def _dil_bias_tiles(rel_bias):
    return pl.pallas_call(
        _dil_bias_kernel,
        grid=(N_DIL, 3, DIL_HEADS),
        in_specs=[pl.BlockSpec(memory_space=pltpu.SMEM)],
        out_specs=pl.BlockSpec((1, 1, 1, TQ_DIL, W_DIL), lambda g, v, h: (g, v, h, 0, 0)),
        out_shape=jax.ShapeDtypeStruct((N_DIL, 3, DIL_HEADS, TQ_DIL, W_DIL), F32),
        compiler_params=_cparams(("arbitrary",) * 3, 32),
        name="dil_bias_tiles",
    )(rel_bias)


_QK_BLOCKS = {0: 0, 1: 1, 2: 2, 3: 3}
for _g in range(N_DIL):
    _QK_BLOCKS[6 + 3 * _g] = 4 + 2 * _g
    _QK_BLOCKS[7 + 3 * _g] = 5 + 2 * _g


def _inproj_kernel(x_ref, mod_ref, g1_ref, w_ref, gain_ref, gmat_ref,
                   qa_ref, ka_ref, va_ref, dil0_ref, dil1_ref, dil2_ref, gate_ref, cls_ref):
    tm = x_ref.shape[1]
    dil_refs = (dil0_ref, dil1_ref, dil2_ref)
    x = x_ref[0]
    mod = mod_ref[0]
    ms = jnp.mean(x * x, axis=-1, keepdims=True)
    h = (x * lax.rsqrt(ms + EPS)) * g1_ref[...]
    h = h * (1.0 + mod[1:2]) + mod[0:1]
    hb = h.astype(BF16)
    gmat = gmat_ref[...]

    for blk in range(N_COL_BLKS):
        acc = jnp.dot(hb, w_ref[:, blk * COL_BLK:(blk + 1) * COL_BLK], preferred_element_type=F32)
        dilation = 1
        if blk < 6:
            dst, off = (qa_ref, ka_ref, va_ref)[blk // 2], (blk % 2) * COL_BLK
        elif blk < 15:
            g = (blk - 6) // 3
            dst, off, dilation = dil_refs[g], ((blk - 6) % 3) * COL_BLK, DIL_CONFIG[g][1]
        else:
            dst, off = gate_ref, (blk - 15) * COL_BLK

        def put(lo, y):
            if dilation == 1:
                dst[0, :, off + lo:off + lo + y.shape[1]] = y.astype(BF16)
            else:
                for k in range(y.shape[1] // LANES):
                    cls_ref[lo // LANES + k] = y[:, k * LANES:(k + 1) * LANES]

        if blk in _QK_BLOCKS:
            gi = _QK_BLOCKS[blk]
            for half in range(COL_BLK // NORM_W):
                lo = half * NORM_W
                a = acc[:, lo:lo + NORM_W]
                hm = jnp.dot((a * a).astype(BF16), gmat, preferred_element_type=F32)
                put(lo, (a * lax.rsqrt(hm + EPS)) * gain_ref[gi:gi + 1, lo:lo + NORM_W])
        elif blk >= 15:
            put(0, 1.0 / (1.0 + jnp.exp(-acc)))
        else:
            put(0, acc)
        if dilation > 1:
            for c in range(dilation):
                rows = [cls_ref[k, pl.ds(c, tm // dilation, stride=dilation), :] for k in range(COL_BLK // LANES)]
                dst[0, c, :, off:off + COL_BLK] = jnp.concatenate(rows, axis=1).astype(BF16)


def _inproj(x, mod, g1, w_in_b, gains, gmat):
    B, S, _ = x.shape
    tm = TM_PROJ
    const = lambda b, i: (0, 0)
    tok = lambda b, i: (b, i, 0)
    dil_shapes = tuple(jax.ShapeDtypeStruct((B, S, 3 * DIL_W) if r == 1 else (B, r, S // r, 3 * DIL_W), BF16)
                       for _, r in DIL_CONFIG)
    dil_specs = tuple(pl.BlockSpec((1, tm, 3 * DIL_W), tok) if r == 1 else
                      pl.BlockSpec((1, r, tm // r, 3 * DIL_W), lambda b, i: (b, 0, i, 0))
                      for _, r in DIL_CONFIG)
    tok_shapes = (jax.ShapeDtypeStruct((B, S, 2 * DA_QK), BF16),
                  jax.ShapeDtypeStruct((B, S, 2 * DA_QK), BF16),
                  jax.ShapeDtypeStruct((B, S, DA_V), BF16))
    gate_shape = jax.ShapeDtypeStruct((B, S, 2 * D_MODEL), BF16)
    tok_spec = lambda s: pl.BlockSpec((1, tm, s.shape[-1]), tok)
    return pl.pallas_call(
        _inproj_kernel,
        grid=(B, S // tm),
        in_specs=[pl.BlockSpec((1, tm, D_MODEL), tok),
                  pl.BlockSpec((1, 6, D_MODEL), lambda b, i: (b, 0, 0)),
                  pl.BlockSpec((1, D_MODEL), const),
                  pl.BlockSpec((D_MODEL, IN_COLS), const, pipeline_mode=pl.Buffered(1)),
                  pl.BlockSpec(gains.shape, const),
                  pl.BlockSpec((NORM_W, NORM_W), const)],
        out_specs=tuple(map(tok_spec, tok_shapes)) + dil_specs + (tok_spec(gate_shape),),
        out_shape=tok_shapes + dil_shapes + (gate_shape,),
        scratch_shapes=[pltpu.VMEM((COL_BLK // LANES, tm, LANES), F32)],
        compiler_params=_cparams(("arbitrary", "arbitrary"), 56),
        name="in_projection",
    )(x, mod, g1, w_in_b, gains, gmat)


def _diffattn_kernel(dac_ref, lam_ref, q_ref, k_ref, v_ref, bias_ref, g_ref, o_ref, vext_ref, acc_ref):
    T = T_ATT
    TQ = q_ref.shape[1]
    n_sub = TQ // T
    h = pl.program_id(1)
    i = pl.program_id(2)
    n_kv = k_ref.shape[1] // T

    @pl.when(i == 0)
    def _():
        vext_ref[:, :DA_VDIM] = v_ref[0]
        vext_ref[:, DA_VDIM:] = jnp.ones((vext_ref.shape[0], DA_VDIM), BF16)

    q = q_ref[0]
    lane = lax.broadcasted_iota(jnp.int32, q.shape, 1)
    zero = jnp.zeros_like(q)
    qs = jnp.concatenate([jnp.where(lane < HEAD_DIM, q, zero), jnp.where(lane >= HEAD_DIM, q, zero)], axis=0)

    def kv_rows(j):
        return pl.ds(pl.multiple_of(j * T, T), T)

    def logits(j):
        s = lax.dot_general(qs, k_ref[0, kv_rows(j), :], (((1,), (1,)), ((), ())),
                            preferred_element_type=F32)
        tiles = [bias_ref[0, jnp.clip(j - (i * n_sub + t), -2, 2) + 2] for t in range(n_sub)]
        return s + jnp.concatenate(tiles + tiles, axis=0)

    fast_ok = dac_ref[h, 3] > 0.5

    @pl.when(fast_ok)
    def _():
        acc_ref[...] = jnp.zeros(acc_ref.shape, F32)

        def body(j, c):
            acc_ref[...] += jnp.dot(jnp.exp2(logits(j)).astype(BF16), vext_ref[kv_rows(j), :],
                                    preferred_element_type=F32)
            return c

        lax.fori_loop(0, n_kv, body, 0, unroll=DA_UNROLL)

    @pl.when(jnp.logical_not(fast_ok))
    def _():
        def body(j, carry):
            m, acc = carry
            u = logits(j)
            m_new = jnp.maximum(m, jnp.max(u, axis=-1, keepdims=True))
            p = jnp.exp2(u - m_new)
            acc = jnp.exp2(m - m_new) * acc + jnp.dot(p.astype(BF16), vext_ref[kv_rows(j), :],
                                                       preferred_element_type=F32)
            return m_new, acc

        init = (jnp.full((2 * TQ, 1), NEG_INF, F32), jnp.zeros(acc_ref.shape, F32))
        acc_ref[...] = lax.fori_loop(0, n_kv, body, init)[1]

    lp = lam_ref[...]
    lam = (jnp.exp(jnp.sum(lp[0:1] * lp[1:2], axis=-1, keepdims=True))
           - jnp.exp(jnp.sum(lp[2:3] * lp[3:4], axis=-1, keepdims=True)) + LAM_INIT)
    o = (acc_ref[:TQ, :DA_VDIM] / acc_ref[:TQ, DA_VDIM:]
         - lam * (acc_ref[TQ:, :DA_VDIM] / acc_ref[TQ:, DA_VDIM:]))
    ms = jnp.mean(o * o, axis=-1, keepdims=True)
    o = (o * lax.rsqrt(ms + EPS)) * g_ref[...] * (1.0 - LAM_INIT)
    o_ref[0] = o.astype(o_ref.dtype)


def _diff_attention(dac, lam_params, qa, ka, va, bias_tiles, subln_g):
    B, S, _ = qa.shape
    T = T_ATT
    TQ = TQ_ATT
    return pl.pallas_call(
        _diffattn_kernel,
        grid=(B, DA_HEADS, S // TQ),
        in_specs=[pl.BlockSpec(memory_space=pltpu.SMEM),
                  pl.BlockSpec((4, HEAD_DIM), lambda b, h, i: (0, 0)),
                  pl.BlockSpec((1, TQ, LANES), lambda b, h, i: (b, i, h)),
                  pl.BlockSpec((1, S, LANES), lambda b, h, i: (b, 0, h)),
                  pl.BlockSpec((1, S, LANES), lambda b, h, i: (b, 0, h)),
                  pl.BlockSpec((1, 5, T, T), lambda b, h, i: (h, 0, 0, 0)),
                  pl.BlockSpec((1, DA_VDIM), lambda b, h, i: (0, 0))],
        out_specs=pl.BlockSpec((1, TQ, LANES), lambda b, h, i: (b, i, h)),
        out_shape=jax.ShapeDtypeStruct((B, S, DA_V), BF16),
        scratch_shapes=[pltpu.VMEM((S, 2 * DA_VDIM), BF16), pltpu.VMEM((2 * TQ, 2 * DA_VDIM), F32)],
        compiler_params=_cparams(("arbitrary",) * 3, 56),
        name="diff_attention",
    )(dac, lam_params, qa, ka, va, bias_tiles, subln_g)


def _dilattn_kernel(q_ref, k_ref, v_ref, bias_ref, o_ref, lse_ref):
    i = pl.program_id(2)
    n_q = pl.num_programs(2)
    L = k_ref.shape[2]
    w0 = pl.multiple_of(jnp.clip(i * TQ_DIL - HALF_DIL, 0, L - W_DIL), HALF_DIL)
    variant = jnp.where(i == 0, 0, jnp.where(i == n_q - 1, 2, 1))

    lane = lax.broadcasted_iota(jnp.int32, (TQ_DIL, LANES), 1)
    low = lane < HEAD_DIM
    for hp in range(DIL_HEADS // 2):
        cs = slice(hp * LANES, (hp + 1) * LANES)
        qp = q_ref[0, 0, :, cs]
        kp = k_ref[0, 0, pl.ds(w0, W_DIL), cs]
        vp = v_ref[0, 0, pl.ds(w0, W_DIL), cs]
        zero = jnp.zeros_like(qp)
        outs, lses = [], []
        for half in range(2):
            qm = jnp.where(low, qp, zero) if half == 0 else jnp.where(low, zero, qp)
            s = lax.dot_general(qm, kp, (((1,), (1,)), ((), ())), preferred_element_type=F32)
            s = s + bias_ref[0, variant, 2 * hp + half]
            m = jnp.max(s, axis=-1, keepdims=True)
            p = jnp.exp(s - m)
            l = jnp.sum(p, axis=-1, keepdims=True)
            outs.append(jnp.dot(p.astype(BF16), vp, preferred_element_type=F32) / l)
            lses.append(m + jnp.log(l))
        o_pair = jnp.where(low, outs[0], outs[1]).astype(o_ref.dtype)
        lse_pair = jnp.where(low, lses[0], lses[1])
        if len(o_ref.shape) == 6:
            n = o_ref.shape[4]
            for t in range(o_ref.shape[1]):
                o_ref[0, t, 0, hp] = o_pair[t * n:(t + 1) * n]
                lse_ref[0, t, 0, hp] = lse_pair[t * n:(t + 1) * n]
        else:
            o_ref[0, 0, :, cs] = o_pair
            lse_ref[0, 0, :, cs] = lse_pair


def _dilated_attention(dil, bias_tiles, g):
    B, r, L, _ = dil.shape
    kv_mode = pl.Buffered(1) if L * DIL_W * 2 > (4 << 20) else pl.Buffered(2)
    qmap = lambda b, c, i: (b, c, i, 0)
    if r == 1:
        o_spec = pl.BlockSpec((1, 1, TQ_DIL, DIL_W), qmap)
        o_shape, o_dtype = (B, r, L, DIL_W), BF16
    else:
        n = TM_OUT // r
        nk = DIL_W // LANES
        o_spec = pl.BlockSpec((1, TQ_DIL // n, 1, nk, n, LANES), lambda b, c, i: (b, i, c, 0, 0, 0))
        o_shape, o_dtype = (B, L // n, r, nk, n, LANES), F32
    return pl.pallas_call(
        _dilattn_kernel,
        grid=(B, r, L // TQ_DIL),
        in_specs=[pl.BlockSpec((1, 1, TQ_DIL, DIL_W), qmap),
                  pl.BlockSpec((1, 1, L, DIL_W), lambda b, c, i: (b, c, 0, 1), pipeline_mode=kv_mode),
                  pl.BlockSpec((1, 1, L, DIL_W), lambda b, c, i: (b, c, 0, 2), pipeline_mode=kv_mode),
                  pl.BlockSpec((1, 3, DIL_HEADS, TQ_DIL, W_DIL), lambda b, c, i: (g, 0, 0, 0, 0))],
        out_specs=(o_spec, o_spec),
        out_shape=(jax.ShapeDtypeStruct(o_shape, o_dtype), jax.ShapeDtypeStruct(o_shape, F32)),
        compiler_params=_cparams(("arbitrary",) * 3, 48),
        name=f"dilated_attention_r{r}",
    )(dil, dil, dil, bias_tiles)


def _interleave_classes(ref, r, il_ref):
    nk = DIL_W // LANES
    n = ref.shape[2] // (r * nk)
    cols = []
    for k in range(nk):
        pieces = [ref[0, 0, pl.ds(k * n + l, r, stride=nk * n), :] for l in range(n)]
        if r % 8 == 0:
            cols.append(jnp.concatenate(pieces, axis=0))
        else:
            for l in range(n):
                il_ref[k, pl.ds(l * r, r), :] = pieces[l]
            cols.append(il_ref[k])
    return jnp.concatenate(cols, axis=1)


def _merge_mlp_kernel(x_ref, mod_ref, ya_ref, o0_ref, l0_ref, o1_ref, l1_ref, o2_ref, l2_ref, gate_ref,
                      wa_ref, wb_ref, wo_ref, g2_ref, wu_ref, wd_ref, out_ref, il_ref):
    mod = mod_ref[0]
    r1, r2 = DIL_CONFIG[1][1], DIL_CONFIG[2][1]
    outs = (o0_ref[0].astype(F32), _interleave_classes(o1_ref, r1, il_ref), _interleave_classes(o2_ref, r2, il_ref))
    lses = (l0_ref[0], _interleave_classes(l1_ref, r1, il_ref), _interleave_classes(l2_ref, r2, il_ref))
    mx = jnp.maximum(jnp.maximum(lses[0], lses[1]), lses[2])
    es = [jnp.exp(l - mx) for l in lses]
    den = es[0] + es[1] + es[2]
    yb = ((es[0] * outs[0] + es[1] * outs[1] + es[2] * outs[2]) / den).astype(BF16)
    pa = jnp.dot(ya_ref[0], wa_ref[...], preferred_element_type=F32)
    pb = jnp.dot(yb, wb_ref[...], preferred_element_type=F32)
    gates = gate_ref[0]
    merged = gates[:, :D_MODEL].astype(F32) * pa + gates[:, D_MODEL:].astype(F32) * pb
    x = x_ref[0] + mod[2:3] * jnp.dot(merged.astype(BF16), wo_ref[...], preferred_element_type=F32)

    ms = jnp.mean(x * x, axis=-1, keepdims=True)
    h = (x * lax.rsqrt(ms + EPS)) * g2_ref[...]
    hb = (h * (1.0 + mod[4:5]) + mod[3:4]).astype(BF16)
    acc = jnp.zeros(x.shape, F32)
    for f in range(D_FF // D_MODEL):
        cs = slice(f * D_MODEL, (f + 1) * D_MODEL)
        u = jnp.maximum(jnp.dot(hb, wu_ref[:, cs], preferred_element_type=F32), 0.0)
        acc = acc + jnp.dot((u * u).astype(BF16), wd_ref[cs, :], preferred_element_type=F32)
    out_ref[0] = x + mod[5:6] * acc


def _merge_mlp(x, mod, ya, o0, l0, o1, l1, o2, l2, gates, wa, wb, wo, g2, wu, wd):
    B, S, _ = x.shape
    tm = TM_OUT
    tok = lambda b, i: (b, i, 0)
    const = lambda b, i: (0, 0)
    tspec = lambda w: pl.BlockSpec((1, tm, w), tok)
    cspec = lambda a: pl.BlockSpec((1, 1) + a.shape[2:], lambda b, i: (b, i, 0, 0))
    wspec = lambda a: pl.BlockSpec(a.shape, const, pipeline_mode=pl.Buffered(1))
    return pl.pallas_call(
        _merge_mlp_kernel,
        grid=(B, S // tm),
        in_specs=[tspec(D_MODEL), pl.BlockSpec((1, 6, D_MODEL), lambda b, i: (b, 0, 0)), tspec(DA_V),
                  tspec(DIL_W), tspec(DIL_W), cspec(o1), cspec(l1), cspec(o2), cspec(l2),
                  tspec(2 * D_MODEL), wspec(wa), wspec(wb), wspec(wo),
                  pl.BlockSpec((1, D_MODEL), const), wspec(wu), wspec(wd)],
        out_specs=tspec(D_MODEL),
        out_shape=jax.ShapeDtypeStruct((B, S, D_MODEL), F32),
        scratch_shapes=[pltpu.VMEM((DIL_W // LANES, tm, LANES), F32)],
        compiler_params=_cparams(("arbitrary", "arbitrary"), 56),
        name="merge_mlp",
    )(x, mod, ya, o0, l0, o1, l1, o2, l2, gates, wa, wb, wo, g2, wu, wd)


def _pair_heads(a, b):
    d = a.shape[0]
    return jnp.stack([a.reshape(d, DA_HEADS, HEAD_DIM), b.reshape(d, DA_HEADS, HEAD_DIM)],
                     axis=2).reshape(d, 2 * DA_QK)


def _layer(x, mod, p):
    B, S, _ = x.shape
    qa, ka, va, dil0, dil1, dil2, gates = _inproj(x, mod, p["g1"], p["w_in"], p["gains"], p["gmat"])
    ya = _diff_attention(p["dac"], p["lam"], qa, ka, va, p["da_bias"], p["subln_g"])
    o0, l0 = _dilated_attention(dil0.reshape(B, 1, S, 3 * DIL_W), p["dil_bias"], 0)
    o1, l1 = _dilated_attention(dil1, p["dil_bias"], 1)
    o2, l2 = _dilated_attention(dil2, p["dil_bias"], 2)
    rows = lambda a: a.reshape(B, a.shape[1], -1, LANES)
    return _merge_mlp(x, mod, ya, o0.reshape(B, S, DIL_W), l0.reshape(B, S, DIL_W),
                      rows(o1), rows(l1), rows(o2), rows(l2), gates,
                      p["w_br_a"], p["w_br_b"], p["w_o"], p["g2"], p["w_up"], p["w_down"])


def kernel(x_prompt, x_sample, c_prompt, c_sample, rel_bias, norm1_g, w_ada, b_ada, w_in, qn_a, kn_a,
           lambda_q1, lambda_k1, lambda_q2, lambda_k2, subln_g, qn_b, kn_b, w_br_a, w_br_b, w_o,
           norm2_g, w_up, w_down):
    nbp, nbs = c_prompt.shape[0], c_sample.shape[0]
    pad = (-(nbp + nbs)) % 8
    c_all = jnp.concatenate([c_prompt, c_sample, jnp.zeros((pad, D_MODEL), F32)], axis=0)
    mod = _modulation(c_all, w_ada[0], b_ada[0]).reshape(-1, 6, D_MODEL)

    w = w_in[0]
    q1, q2, k1, k2 = (w[:, n * DA_QK:(n + 1) * DA_QK] for n in range(4))
    w_perm = jnp.concatenate([_pair_heads(q1, q2), _pair_heads(k1, k2), w[:, 4 * DA_QK:]], axis=1)

    scale = HEAD_DIM ** -0.5
    tile8 = lambda v: jnp.tile(v, DIL_W // HEAD_DIM)
    gain_rows = [tile8(qn_a[0]) * (scale * LOG2E)] * 2 + [tile8(kn_a[0])] * 2
    for g in range(N_DIL):
        gain_rows += [tile8(qn_b[0, g]) * scale, tile8(kn_b[0, g])]
    ids = jnp.arange(NORM_W) // HEAD_DIM
    gmat = jnp.where(ids[:, None] == ids[None, :], 1.0 / HEAD_DIM, 0.0).astype(BF16)

    da_tab = rel_bias[:, :DA_HEADS]
    qk_bound = HEAD_DIM * scale * jnp.max(jnp.abs(qn_a[0])) * jnp.max(jnp.abs(kn_a[0])) * BF16_ROUND_MARGIN
    c_ref = LOG2E * (qk_bound + jnp.max(da_tab, axis=0)) - DA_HEADROOM
    u_min = LOG2E * (jnp.min(da_tab, axis=0) - qk_bound) - c_ref
    dac = jnp.stack([c_ref, c_ref - LOG2E * da_tab[N_BUCKETS // 2 - 1], c_ref - LOG2E * da_tab[N_BUCKETS - 1],
                     (u_min >= DA_MIN_EXP).astype(F32)], axis=1)

    p = {
        "g1": norm1_g[0].reshape(1, D_MODEL),
        "g2": norm2_g[0].reshape(1, D_MODEL),
        "w_in": w_perm.astype(BF16),
        "gains": jnp.stack(gain_rows, axis=0),
        "gmat": gmat,
        "lam": jnp.stack([lambda_q1[0], lambda_k1[0], lambda_q2[0], lambda_k2[0]], axis=0),
        "subln_g": subln_g[0].reshape(1, DA_VDIM),
        "dac": dac,
        "da_bias": _da_bias_tiles(rel_bias, dac),
        "dil_bias": _dil_bias_tiles(rel_bias),
        "w_br_a": w_br_a[0].astype(BF16),
        "w_br_b": w_br_b[0].astype(BF16),
        "w_o": w_o[0].astype(BF16),
        "w_up": w_up[0].astype(BF16),
        "w_down": w_down[0].astype(BF16),
    }
    y_prompt = _layer(x_prompt, mod[:nbp], p)
    y_sample = _layer(x_sample, mod[nbp:nbp + nbs], p)
    return (y_prompt, y_sample)
```

```python
import functools
import math

import jax
import jax.numpy as jnp
from jax import lax
from jax.experimental import pallas as pl
from jax.experimental.pallas import tpu as pltpu

F32 = jnp.float32
BF16 = jnp.bfloat16

D_MODEL = 1024
HEAD_DIM = 64
DA_HEADS = 8
DA_VDIM = 2 * HEAD_DIM
DIL_CONFIG = ((128, 1), (512, 4), (2048, 16))
N_DIL = 3
DIL_HEADS = 8
D_FF = 4 * D_MODEL
N_BUCKETS = 32
NEG_INF = -1e30
EPS = 1e-6
LAM_INIT = 0.8 - 0.6 * math.exp(-0.3 * 0)

DA_QK = DA_HEADS * HEAD_DIM
DA_V = DA_HEADS * DA_VDIM
DIL_W = DIL_HEADS * HEAD_DIM
DIL_COLS = 3 * N_DIL * DIL_W
IN_COLS = 4 * DA_QK + DA_V + DIL_COLS + 2 * D_MODEL

LANES = 128
COL_BLK = 512
N_COL_BLKS = IN_COLS // COL_BLK
NORM_W = 256

TM_PROJ = 512
TM_OUT = 256
T_ATT = 512
TQ_ATT = 1024
TQ_DIL = 128
HALF_DIL = 64
W_DIL = TQ_DIL + 2 * HALF_DIL

LOG2E = math.log2(math.e)
LN2 = math.log(2.0)
DA_HEADROOM = 64.0
DA_MIN_EXP = -60.0
DA_UNROLL = 4
BF16_ROUND_MARGIN = 1.02

_BUCKET_STEPS = (12, 16, 23, 32, 46, 64, 91)


def _cparams(sem, vmem_mb):
    return pltpu.CompilerParams(dimension_semantics=sem, vmem_limit_bytes=vmem_mb * 1024 * 1024)


def _mod_kernel(c_ref, w_ref, b_ref, o_ref):
    c = c_ref[...]
    a = c * (1.0 / (1.0 + jnp.exp(-c)))
    o_ref[...] = jnp.dot(a, w_ref[...], preferred_element_type=F32,
                         precision=lax.Precision.HIGHEST) + b_ref[...]


def _modulation(c_all, w_ada, b_ada):
    rows = c_all.shape[0]
    n = w_ada.shape[1]
    return pl.pallas_call(
        _mod_kernel,
        grid=(n // D_MODEL,),
        in_specs=[pl.BlockSpec((rows, D_MODEL), lambda j: (0, 0)),
                  pl.BlockSpec((D_MODEL, D_MODEL), lambda j: (0, j)),
                  pl.BlockSpec((1, D_MODEL), lambda j: (0, j))],
        out_specs=pl.BlockSpec((rows, D_MODEL), lambda j: (0, j)),
        out_shape=jax.ShapeDtypeStruct((rows, n), F32),
        compiler_params=_cparams(("arbitrary",), 32),
        name="modulation",
    )(c_all, w_ada, b_ada.reshape(1, n))


def _bias_lookup(rel, tab_ref, col):
    n = jnp.abs(rel)
    large = jnp.full(rel.shape, 8, jnp.int32)
    for t in _BUCKET_STEPS:
        large = large + (n >= t).astype(jnp.int32)
    bucket = jnp.where(n < 8, n, large) + jnp.where(rel > 0, N_BUCKETS // 2, 0)
    val = jnp.zeros(rel.shape, F32)
    for b in range(N_BUCKETS):
        val = jnp.where(bucket == b, tab_ref[b, col], val)
    return val


def _da_bias_kernel(tab_ref, dac_ref, o_ref):
    h = pl.program_id(0)
    sat = _BUCKET_STEPS[-1]
    row = lax.broadcasted_iota(jnp.int32, (LANES, LANES), 0)
    col = lax.broadcasted_iota(jnp.int32, (LANES, LANES), 1)
    far = (jnp.full((LANES, LANES), -dac_ref[h, 1], F32), jnp.full((LANES, LANES), -dac_ref[h, 2], F32))
    for d in range(-2, 3):
        for br in range(T_ATT // LANES):
            for bc in range(T_ATT // LANES):
                base = d * T_ATT + (bc - br) * LANES
                if base + LANES - 1 <= -sat:
                    val = far[0]
                elif base - (LANES - 1) >= sat:
                    val = far[1]
                else:
                    val = LOG2E * _bias_lookup(col - row + base, tab_ref, h) - dac_ref[h, 0]
                o_ref[0, d + 2, br * LANES:(br + 1) * LANES, bc * LANES:(bc + 1) * LANES] = val


def _da_bias_tiles(rel_bias, dac):
    assert T_ATT >= _BUCKET_STEPS[-1]
    return pl.pallas_call(
        _da_bias_kernel,
        grid=(DA_HEADS,),
        in_specs=[pl.BlockSpec(memory_space=pltpu.SMEM), pl.BlockSpec(memory_space=pltpu.SMEM)],
        out_specs=pl.BlockSpec((1, 5, T_ATT, T_ATT), lambda h: (h, 0, 0, 0)),
        out_shape=jax.ShapeDtypeStruct((DA_HEADS, 5, T_ATT, T_ATT), F32),
        compiler_params=_cparams(("arbitrary",), 32),
        name="da_bias_tiles",
    )(rel_bias, dac)


def _dil_bias_kernel(tab_ref, dilc_ref, o_ref):
    g = pl.program_id(0)
    v = pl.program_id(1)
    h = pl.program_id(2)
    row = lax.broadcasted_iota(jnp.int32, (TQ_DIL, W_DIL), 0)
    col = lax.broadcasted_iota(jnp.int32, (TQ_DIL, W_DIL), 1)
    rel = col - row - HALF_DIL * v
    dilation = lax.shift_left(jnp.int32(1), 2 * g)
    val = LOG2E * _bias_lookup(rel * dilation, tab_ref, DA_HEADS + g * DIL_HEADS + h) - dilc_ref[g, h, 0]
    o_ref[0, 0, 0] = jnp.where(jnp.abs(rel) <= HALF_DIL, val, NEG_INF)


def _dil_bias_tiles(rel_bias, dilc):
    return pl.pallas_call(
        _dil_bias_kernel,
        grid=(N_DIL, 3, DIL_HEADS),
        in_specs=[pl.BlockSpec(memory_space=pltpu.SMEM), pl.BlockSpec(memory_space=pltpu.SMEM)],
        out_specs=pl.BlockSpec((1, 1, 1, TQ_DIL, W_DIL), lambda g, v, h: (g, v, h, 0, 0)),
        out_shape=jax.ShapeDtypeStruct((N_DIL, 3, DIL_HEADS, TQ_DIL, W_DIL), F32),
        compiler_params=_cparams(("arbitrary",) * 3, 32),
        name="dil_bias_tiles",
    )(rel_bias, dilc)


_QK_BLOCKS = {0: 0, 1: 1, 2: 2, 3: 3}
for _g in range(N_DIL):
    _QK_BLOCKS[6 + 3 * _g] = 4 + 2 * _g
    _QK_BLOCKS[7 + 3 * _g] = 5 + 2 * _g


def _inproj_kernel(x_ref, mod_ref, g1_ref, w_ref, gain_ref, gmat_ref,
                   qa_ref, ka_ref, va_ref, dil0_ref, dil1_ref, dil2_ref, gate_ref, cls_ref):
    tm = x_ref.shape[1]
    dil_refs = (dil0_ref, dil1_ref, dil2_ref)
    x = x_ref[0]
    mod = mod_ref[0]
    ms = jnp.mean(x * x, axis=-1, keepdims=True)
    h = (x * lax.rsqrt(ms + EPS)) * g1_ref[...]
    h = h * (1.0 + mod[1:2]) + mod[0:1]
    hb = h.astype(BF16)
    gmat = gmat_ref[...]

    for blk in range(N_COL_BLKS):
        acc = jnp.dot(hb, w_ref[:, blk * COL_BLK:(blk + 1) * COL_BLK], preferred_element_type=F32)
        dilation = 1
        if blk < 6:
            dst, off = (qa_ref, ka_ref, va_ref)[blk // 2], (blk % 2) * COL_BLK
        elif blk < 15:
            g = (blk - 6) // 3
            dst, off, dilation = dil_refs[g], ((blk - 6) % 3) * COL_BLK, DIL_CONFIG[g][1]
        else:
            dst, off = gate_ref, (blk - 15) * COL_BLK

        def put(lo, y):
            if dilation == 1:
                dst[0, :, off + lo:off + lo + y.shape[1]] = y.astype(BF16)
            else:
                for k in range(y.shape[1] // LANES):
                    cls_ref[lo // LANES + k] = y[:, k * LANES:(k + 1) * LANES]

        if blk in _QK_BLOCKS:
            gi = _QK_BLOCKS[blk]
            for half in range(COL_BLK // NORM_W):
                lo = half * NORM_W
                a = acc[:, lo:lo + NORM_W]
                hm = jnp.dot((a * a).astype(BF16), gmat, preferred_element_type=F32)
                put(lo, (a * lax.rsqrt(hm + EPS)) * gain_ref[gi:gi + 1, lo:lo + NORM_W])
        elif blk >= 15:
            put(0, 1.0 / (1.0 + jnp.exp(-acc)))
        else:
            put(0, acc)
        if dilation > 1:
            for c in range(dilation):
                rows = [cls_ref[k, pl.ds(c, tm // dilation, stride=dilation), :] for k in range(COL_BLK // LANES)]
                dst[0, c, :, off:off + COL_BLK] = jnp.concatenate(rows, axis=1).astype(BF16)


def _inproj(x, mod, g1, w_in_b, gains, gmat):
    B, S, _ = x.shape
    tm = TM_PROJ
    const = lambda b, i: (0, 0)
    tok = lambda b, i: (b, i, 0)
    dil_shapes = tuple(jax.ShapeDtypeStruct((B, S, 3 * DIL_W) if r == 1 else (B, r, S // r, 3 * DIL_W), BF16)
                       for _, r in DIL_CONFIG)
    dil_specs = tuple(pl.BlockSpec((1, tm, 3 * DIL_W), tok) if r == 1 else
                      pl.BlockSpec((1, r, tm // r, 3 * DIL_W), lambda b, i: (b, 0, i, 0))
                      for _, r in DIL_CONFIG)
    tok_shapes = (jax.ShapeDtypeStruct((B, S, 2 * DA_QK), BF16),
                  jax.ShapeDtypeStruct((B, S, 2 * DA_QK), BF16),
                  jax.ShapeDtypeStruct((B, S, DA_V), BF16))
    gate_shape = jax.ShapeDtypeStruct((B, S, 2 * D_MODEL), BF16)
    tok_spec = lambda s: pl.BlockSpec((1, tm, s.shape[-1]), tok)
    return pl.pallas_call(
        _inproj_kernel,
        grid=(B, S // tm),
        in_specs=[pl.BlockSpec((1, tm, D_MODEL), tok),
                  pl.BlockSpec((1, 6, D_MODEL), lambda b, i: (b, 0, 0)),
                  pl.BlockSpec((1, D_MODEL), const),
                  pl.BlockSpec((D_MODEL, IN_COLS), const, pipeline_mode=pl.Buffered(1)),
                  pl.BlockSpec(gains.shape, const),
                  pl.BlockSpec((NORM_W, NORM_W), const)],
        out_specs=tuple(map(tok_spec, tok_shapes)) + dil_specs + (tok_spec(gate_shape),),
        out_shape=tok_shapes + dil_shapes + (gate_shape,),
        scratch_shapes=[pltpu.VMEM((COL_BLK // LANES, tm, LANES), F32)],
        compiler_params=_cparams(("arbitrary", "arbitrary"), 56),
        name="in_projection",
    )(x, mod, g1, w_in_b, gains, gmat)


def _diffattn_kernel(dac_ref, lam_ref, q_ref, k_ref, v_ref, bias_ref, g_ref, o_ref, vext_ref, acc_ref):
    T = T_ATT
    TQ = q_ref.shape[1]
    n_sub = TQ // T
    h = pl.program_id(1)
    i = pl.program_id(2)
    n_kv = k_ref.shape[1] // T

    @pl.when(i == 0)
    def _():
        vext_ref[:, :DA_VDIM] = v_ref[0]
        vext_ref[:, DA_VDIM:] = jnp.ones((vext_ref.shape[0], DA_VDIM), BF16)

    q = q_ref[0]
    lane = lax.broadcasted_iota(jnp.int32, q.shape, 1)
    zero = jnp.zeros_like(q)
    qs = jnp.concatenate([jnp.where(lane < HEAD_DIM, q, zero), jnp.where(lane >= HEAD_DIM, q, zero)], axis=0)

    def kv_rows(j):
        return pl.ds(pl.multiple_of(j * T, T), T)

    def logits(j):
        s = lax.dot_general(qs, k_ref[0, kv_rows(j), :], (((1,), (1,)), ((), ())),
                            preferred_element_type=F32)
        tiles = [bias_ref[0, jnp.clip(j - (i * n_sub + t), -2, 2) + 2] for t in range(n_sub)]
        return s + jnp.concatenate(tiles + tiles, axis=0)

    fast_ok = dac_ref[h, 3] > 0.5

    @pl.when(fast_ok)
    def _():
        acc_ref[...] = jnp.zeros(acc_ref.shape, F32)

        def body(j, c):
            acc_ref[...] += jnp.dot(jnp.exp2(logits(j)).astype(BF16), vext_ref[kv_rows(j), :],
                                    preferred_element_type=F32)
            return c

        lax.fori_loop(0, n_kv, body, 0, unroll=DA_UNROLL)

    @pl.when(jnp.logical_not(fast_ok))
    def _():
        def body(j, carry):
            m, acc = carry
            u = logits(j)
            m_new = jnp.maximum(m, jnp.max(u, axis=-1, keepdims=True))
            p = jnp.exp2(u - m_new)
            acc = jnp.exp2(m - m_new) * acc + jnp.dot(p.astype(BF16), vext_ref[kv_rows(j), :],
                                                       preferred_element_type=F32)
            return m_new, acc

        init = (jnp.full((2 * TQ, 1), NEG_INF, F32), jnp.zeros(acc_ref.shape, F32))
        acc_ref[...] = lax.fori_loop(0, n_kv, body, init)[1]

    lp = lam_ref[...]
    lam = (jnp.exp(jnp.sum(lp[0:1] * lp[1:2], axis=-1, keepdims=True))
           - jnp.exp(jnp.sum(lp[2:3] * lp[3:4], axis=-1, keepdims=True)) + LAM_INIT)
    o = (acc_ref[:TQ, :DA_VDIM] / acc_ref[:TQ, DA_VDIM:]
         - lam * (acc_ref[TQ:, :DA_VDIM] / acc_ref[TQ:, DA_VDIM:]))
    ms = jnp.mean(o * o, axis=-1, keepdims=True)
    o = (o * lax.rsqrt(ms + EPS)) * g_ref[...] * (1.0 - LAM_INIT)
    o_ref[0] = o.astype(o_ref.dtype)


def _diff_attention(dac, lam_params, qa, ka, va, bias_tiles, subln_g):
    B, S, _ = qa.shape
    T = T_ATT
    TQ = TQ_ATT
    return pl.pallas_call(
        _diffattn_kernel,
        grid=(B, DA_HEADS, S // TQ),
        in_specs=[pl.BlockSpec(memory_space=pltpu.SMEM),
                  pl.BlockSpec((4, HEAD_DIM), lambda b, h, i: (0, 0)),
                  pl.BlockSpec((1, TQ, LANES), lambda b, h, i: (b, i, h)),
                  pl.BlockSpec((1, S, LANES), lambda b, h, i: (b, 0, h)),
                  pl.BlockSpec((1, S, LANES), lambda b, h, i: (b, 0, h)),
                  pl.BlockSpec((1, 5, T, T), lambda b, h, i: (h, 0, 0, 0)),
                  pl.BlockSpec((1, DA_VDIM), lambda b, h, i: (0, 0))],
        out_specs=pl.BlockSpec((1, TQ, LANES), lambda b, h, i: (b, i, h)),
        out_shape=jax.ShapeDtypeStruct((B, S, DA_V), BF16),
        scratch_shapes=[pltpu.VMEM((S, 2 * DA_VDIM), BF16), pltpu.VMEM((2 * TQ, 2 * DA_VDIM), F32)],
        compiler_params=_cparams(("arbitrary",) * 3, 56),
        name="diff_attention",
    )(dac, lam_params, qa, ka, va, bias_tiles, subln_g)


def _dilattn_kernel(g, dilc_ref, q_ref, k_ref, v_ref, bias_ref, o_ref, lse_ref):
    TQ = TQ_DIL
    i = pl.program_id(2)
    n_q = pl.num_programs(2)
    L = k_ref.shape[2]
    w0 = pl.multiple_of(jnp.clip(i * TQ - HALF_DIL, 0, L - W_DIL), HALF_DIL)
    variant = jnp.where(i == 0, 0, jnp.where(i == n_q - 1, 2, 1))
    low = lax.broadcasted_iota(jnp.int32, (TQ, LANES), 1) < HEAD_DIM
    ones = jnp.ones((W_DIL, LANES), BF16)

    def run(use_max):
        for hp in range(DIL_HEADS // 2):
            cs = slice(hp * LANES, (hp + 1) * LANES)
            qp = q_ref[0, 0, :, cs]
            kp = k_ref[0, 0, pl.ds(w0, W_DIL), cs]
            vext = jnp.concatenate([v_ref[0, 0, pl.ds(w0, W_DIL), cs], ones], axis=1)
            zero = jnp.zeros_like(qp)
            qs = jnp.concatenate([jnp.where(low, qp, zero), jnp.where(low, zero, qp)], axis=0)
            u = lax.dot_general(qs, kp, (((1,), (1,)), ((), ())), preferred_element_type=F32)
            u = u + jnp.concatenate([bias_ref[0, variant, 2 * hp], bias_ref[0, variant, 2 * hp + 1]], axis=0)
            if use_max:
                m = jnp.max(u, axis=-1, keepdims=True)
                u = u - m
            r = jnp.dot(jnp.exp2(u).astype(BF16), vext, preferred_element_type=F32)
            num, den = r[:, :LANES], r[:, LANES:]
            lse2 = jnp.log2(den)
            if use_max:
                lse2 = lse2 + m
            o_pair = jnp.where(low, num[:TQ] / den[:TQ], num[TQ:] / den[TQ:]).astype(o_ref.dtype)
            lse_pair = LN2 * jnp.where(low, lse2[:TQ] + dilc_ref[g, 2 * hp, 0], lse2[TQ:] + dilc_ref[g, 2 * hp + 1, 0])
            if len(o_ref.shape) == 6:
                n = o_ref.shape[4]
                for t in range(o_ref.shape[1]):
                    o_ref[0, t, 0, hp] = o_pair[t * n:(t + 1) * n]
                    lse_ref[0, t, 0, hp] = lse_pair[t * n:(t + 1) * n]
            else:
                o_ref[0, 0, :, cs] = o_pair
                lse_ref[0, 0, :, cs] = lse_pair

    fast_ok = dilc_ref[g, 0, 1] > 0.5
    pl.when(fast_ok)(functools.partial(run, False))
    pl.when(jnp.logical_not(fast_ok))(functools.partial(run, True))


def _dilated_attention(dilc, dil, bias_tiles, g):
    B, r, L, _ = dil.shape
    kv_mode = pl.Buffered(1) if L * DIL_W * 2 > (4 << 20) else pl.Buffered(2)
    qmap = lambda b, c, i: (b, c, i, 0)
    if r == 1:
        o_spec = pl.BlockSpec((1, 1, TQ_DIL, DIL_W), qmap)
        o_shape, o_dtype = (B, r, L, DIL_W), BF16
    else:
        n = TM_OUT // r
        nk = DIL_W // LANES
        o_spec = pl.BlockSpec((1, TQ_DIL // n, 1, nk, n, LANES), lambda b, c, i: (b, i, c, 0, 0, 0))
        o_shape, o_dtype = (B, L // n, r, nk, n, LANES), F32
    return pl.pallas_call(
        functools.partial(_dilattn_kernel, g),
        grid=(B, r, L // TQ_DIL),
        in_specs=[pl.BlockSpec(memory_space=pltpu.SMEM),
                  pl.BlockSpec((1, 1, TQ_DIL, DIL_W), qmap),
                  pl.BlockSpec((1, 1, L, DIL_W), lambda b, c, i: (b, c, 0, 1), pipeline_mode=kv_mode),
                  pl.BlockSpec((1, 1, L, DIL_W), lambda b, c, i: (b, c, 0, 2), pipeline_mode=kv_mode),
                  pl.BlockSpec((1, 3, DIL_HEADS, TQ_DIL, W_DIL), lambda b, c, i: (g, 0, 0, 0, 0))],
        out_specs=(o_spec, o_spec),
        out_shape=(jax.ShapeDtypeStruct(o_shape, o_dtype), jax.ShapeDtypeStruct(o_shape, F32)),
        compiler_params=_cparams(("arbitrary",) * 3, 48),
        name=f"dilated_attention_r{r}",
    )(dilc, dil, dil, dil, bias_tiles)


def _interleave_classes(ref, r, il_ref):
    nk = DIL_W // LANES
    n = ref.shape[2] // (r * nk)
    cols = []
    for k in range(nk):
        pieces = [ref[0, 0, pl.ds(k * n + l, r, stride=nk * n), :] for l in range(n)]
        if r % 8 == 0:
            cols.append(jnp.concatenate(pieces, axis=0))
        else:
            for l in range(n):
                il_ref[k, pl.ds(l * r, r), :] = pieces[l]
            cols.append(il_ref[k])
    return jnp.concatenate(cols, axis=1)


def _merge_mlp_kernel(x_ref, mod_ref, ya_ref, o0_ref, l0_ref, o1_ref, l1_ref, o2_ref, l2_ref, gate_ref,
                      wa_ref, wb_ref, wo_ref, g2_ref, wu_ref, wd_ref, out_ref, il_ref):
    mod = mod_ref[0]
    r1, r2 = DIL_CONFIG[1][1], DIL_CONFIG[2][1]
    outs = (o0_ref[0].astype(F32), _interleave_classes(o1_ref, r1, il_ref), _interleave_classes(o2_ref, r2, il_ref))
    lses = (l0_ref[0], _interleave_classes(l1_ref, r1, il_ref), _interleave_classes(l2_ref, r2, il_ref))
    mx = jnp.maximum(jnp.maximum(lses[0], lses[1]), lses[2])
    es = [jnp.exp(l - mx) for l in lses]
    den = es[0] + es[1] + es[2]
    yb = ((es[0] * outs[0] + es[1] * outs[1] + es[2] * outs[2]) / den).astype(BF16)
    pa = jnp.dot(ya_ref[0], wa_ref[...], preferred_element_type=F32)
    pb = jnp.dot(yb, wb_ref[...], preferred_element_type=F32)
    gates = gate_ref[0]
    merged = gates[:, :D_MODEL].astype(F32) * pa + gates[:, D_MODEL:].astype(F32) * pb
    x = x_ref[0] + mod[2:3] * jnp.dot(merged.astype(BF16), wo_ref[...], preferred_element_type=F32)

    ms = jnp.mean(x * x, axis=-1, keepdims=True)
    h = (x * lax.rsqrt(ms + EPS)) * g2_ref[...]
    hb = (h * (1.0 + mod[4:5]) + mod[3:4]).astype(BF16)
    acc = jnp.zeros(x.shape, F32)
    for f in range(D_FF // D_MODEL):
        cs = slice(f * D_MODEL, (f + 1) * D_MODEL)
        u = jnp.maximum(jnp.dot(hb, wu_ref[:, cs], preferred_element_type=F32), 0.0)
        acc = acc + jnp.dot((u * u).astype(BF16), wd_ref[cs, :], preferred_element_type=F32)
    out_ref[0] = x + mod[5:6] * acc


def _merge_mlp(x, mod, ya, o0, l0, o1, l1, o2, l2, gates, wa, wb, wo, g2, wu, wd):
    B, S, _ = x.shape
    tm = TM_OUT
    tok = lambda b, i: (b, i, 0)
    const = lambda b, i: (0, 0)
    tspec = lambda w: pl.BlockSpec((1, tm, w), tok)
    cspec = lambda a: pl.BlockSpec((1, 1) + a.shape[2:], lambda b, i: (b, i, 0, 0))
    wspec = lambda a: pl.BlockSpec(a.shape, const, pipeline_mode=pl.Buffered(1))
    return pl.pallas_call(
        _merge_mlp_kernel,
        grid=(B, S // tm),
        in_specs=[tspec(D_MODEL), pl.BlockSpec((1, 6, D_MODEL), lambda b, i: (b, 0, 0)), tspec(DA_V),
                  tspec(DIL_W), tspec(DIL_W), cspec(o1), cspec(l1), cspec(o2), cspec(l2),
                  tspec(2 * D_MODEL), wspec(wa), wspec(wb), wspec(wo),
                  pl.BlockSpec((1, D_MODEL), const), wspec(wu), wspec(wd)],
        out_specs=tspec(D_MODEL),
        out_shape=jax.ShapeDtypeStruct((B, S, D_MODEL), F32),
        scratch_shapes=[pltpu.VMEM((DIL_W // LANES, tm, LANES), F32)],
        compiler_params=_cparams(("arbitrary", "arbitrary"), 56),
        name="merge_mlp",
    )(x, mod, ya, o0, l0, o1, l1, o2, l2, gates, wa, wb, wo, g2, wu, wd)


def _pair_heads(a, b):
    d = a.shape[0]
    return jnp.stack([a.reshape(d, DA_HEADS, HEAD_DIM), b.reshape(d, DA_HEADS, HEAD_DIM)],
                     axis=2).reshape(d, 2 * DA_QK)


def _layer(x, mod, p):
    B, S, _ = x.shape
    qa, ka, va, dil0, dil1, dil2, gates = _inproj(x, mod, p["g1"], p["w_in"], p["gains"], p["gmat"])
    ya = _diff_attention(p["dac"], p["lam"], qa, ka, va, p["da_bias"], p["subln_g"])
    o0, l0 = _dilated_attention(p["dilc"], dil0.reshape(B, 1, S, 3 * DIL_W), p["dil_bias"], 0)
    o1, l1 = _dilated_attention(p["dilc"], dil1, p["dil_bias"], 1)
    o2, l2 = _dilated_attention(p["dilc"], dil2, p["dil_bias"], 2)
    rows = lambda a: a.reshape(B, a.shape[1], -1, LANES)
    return _merge_mlp(x, mod, ya, o0.reshape(B, S, DIL_W), l0.reshape(B, S, DIL_W),
                      rows(o1), rows(l1), rows(o2), rows(l2), gates,
                      p["w_br_a"], p["w_br_b"], p["w_o"], p["g2"], p["w_up"], p["w_down"])


def kernel(x_prompt, x_sample, c_prompt, c_sample, rel_bias, norm1_g, w_ada, b_ada, w_in, qn_a, kn_a,
           lambda_q1, lambda_k1, lambda_q2, lambda_k2, subln_g, qn_b, kn_b, w_br_a, w_br_b, w_o,
           norm2_g, w_up, w_down):
    nbp, nbs = c_prompt.shape[0], c_sample.shape[0]
    pad = (-(nbp + nbs)) % 8
    c_all = jnp.concatenate([c_prompt, c_sample, jnp.zeros((pad, D_MODEL), F32)], axis=0)
    mod = _modulation(c_all, w_ada[0], b_ada[0]).reshape(-1, 6, D_MODEL)

    w = w_in[0]
    q1, q2, k1, k2 = (w[:, n * DA_QK:(n + 1) * DA_QK] for n in range(4))
    w_perm = jnp.concatenate([_pair_heads(q1, q2), _pair_heads(k1, k2), w[:, 4 * DA_QK:]], axis=1)

    scale = HEAD_DIM ** -0.5
    tile8 = lambda v: jnp.tile(v, DIL_W // HEAD_DIM)
    gain_rows = [tile8(qn_a[0]) * (scale * LOG2E)] * 2 + [tile8(kn_a[0])] * 2
    for g in range(N_DIL):
        gain_rows += [tile8(qn_b[0, g]) * (scale * LOG2E), tile8(kn_b[0, g])]
    ids = jnp.arange(NORM_W) // HEAD_DIM
    gmat = jnp.where(ids[:, None] == ids[None, :], 1.0 / HEAD_DIM, 0.0).astype(BF16)

    def logit_reference(q_gain, k_gain, tab):
        qk_bound = HEAD_DIM * scale * jnp.max(jnp.abs(q_gain)) * jnp.max(jnp.abs(k_gain)) * BF16_ROUND_MARGIN
        c_ref = LOG2E * (qk_bound + jnp.max(tab, axis=0)) - DA_HEADROOM
        u_min = LOG2E * (jnp.min(tab, axis=0) - qk_bound) - c_ref
        return c_ref, jnp.all(u_min >= DA_MIN_EXP).astype(F32) * jnp.ones_like(c_ref)

    da_tab = rel_bias[:, :DA_HEADS]
    c_ref, da_ok = logit_reference(qn_a[0], kn_a[0], da_tab)
    dac = jnp.stack([c_ref, c_ref - LOG2E * da_tab[N_BUCKETS // 2 - 1], c_ref - LOG2E * da_tab[N_BUCKETS - 1],
                     da_ok], axis=1)
    dilc = jnp.stack([jnp.stack(logit_reference(
        qn_b[0, g], kn_b[0, g], rel_bias[:, DA_HEADS + g * DIL_HEADS:DA_HEADS + (g + 1) * DIL_HEADS]), axis=1)
        for g in range(N_DIL)], axis=0)

    p = {
        "g1": norm1_g[0].reshape(1, D_MODEL),
        "g2": norm2_g[0].reshape(1, D_MODEL),
        "w_in": w_perm.astype(BF16),
        "gains": jnp.stack(gain_rows, axis=0),
        "gmat": gmat,
        "lam": jnp.stack([lambda_q1[0], lambda_k1[0], lambda_q2[0], lambda_k2[0]], axis=0),
        "subln_g": subln_g[0].reshape(1, DA_VDIM),
        "dac": dac,
        "da_bias": _da_bias_tiles(rel_bias, dac),
        "dilc": dilc,
        "dil_bias": _dil_bias_tiles(rel_bias, dilc),
        "w_br_a": w_br_a[0].astype(BF16),
        "w_br_b": w_br_b[0].astype(BF16),
        "w_o": w_o[0].astype(BF16),
        "w_up": w_up[0].astype(BF16),
        "w_down": w_down[0].astype(BF16),
    }
    y_prompt = _layer(x_prompt, mod[:nbp], p)
    y_sample = _layer(x_sample, mod[nbp:nbp + nbs], p)
    return (y_prompt, y_sample)
```

```python
import functools
import math

import jax
import jax.numpy as jnp
from jax import lax
from jax.experimental import pallas as pl
from jax.experimental.pallas import tpu as pltpu

F32 = jnp.float32
BF16 = jnp.bfloat16

D_MODEL = 1024
HEAD_DIM = 64
DA_HEADS = 8
DA_VDIM = 2 * HEAD_DIM
DIL_CONFIG = ((128, 1), (512, 4), (2048, 16))
N_DIL = 3
DIL_HEADS = 8
D_FF = 4 * D_MODEL
N_BUCKETS = 32
NEG_INF = -1e30
EPS = 1e-6
LAM_INIT = 0.8 - 0.6 * math.exp(-0.3 * 0)

DA_QK = DA_HEADS * HEAD_DIM
DA_V = DA_HEADS * DA_VDIM
DIL_W = DIL_HEADS * HEAD_DIM
DIL_COLS = 3 * N_DIL * DIL_W
IN_COLS = 4 * DA_QK + DA_V + DIL_COLS + 2 * D_MODEL

LANES = 128
SUBLANES = 8
COL_BLK = 512
N_COL_BLKS = IN_COLS // COL_BLK
NORM_W = 256

TM_PROJ = 512
TM_OUT = 256
T_ATT = 512
TQ_ATT = 1024
TQ_DIL = 128
ROWS_DIL = 512
HALF_DIL = 64
W_DIL = TQ_DIL + 2 * HALF_DIL

LOG2E = math.log2(math.e)
LN2 = math.log(2.0)
DA_HEADROOM = 64.0
DA_MIN_EXP = -60.0
DA_UNROLL = 8
BF16_ROUND_MARGIN = 1.02

_BUCKET_STEPS = (12, 16, 23, 32, 46, 64, 91)


def _cparams(sem, vmem_mb):
    return pltpu.CompilerParams(dimension_semantics=sem, vmem_limit_bytes=vmem_mb * 1024 * 1024)


def _mod_kernel(c_ref, w_ref, b_ref, o_ref):
    c = c_ref[...]
    a = c * (1.0 / (1.0 + jnp.exp(-c)))
    o_ref[...] = jnp.dot(a, w_ref[...], preferred_element_type=F32,
                         precision=lax.Precision.HIGHEST) + b_ref[...]


def _modulation(c_all, w_ada, b_ada):
    rows = c_all.shape[0]
    n = w_ada.shape[1]
    return pl.pallas_call(
        _mod_kernel,
        grid=(n // D_MODEL,),
        in_specs=[pl.BlockSpec((rows, D_MODEL), lambda j: (0, 0)),
                  pl.BlockSpec((D_MODEL, D_MODEL), lambda j: (0, j)),
                  pl.BlockSpec((1, D_MODEL), lambda j: (0, j))],
        out_specs=pl.BlockSpec((rows, D_MODEL), lambda j: (0, j)),
        out_shape=jax.ShapeDtypeStruct((rows, n), F32),
        compiler_params=_cparams(("arbitrary",), 32),
        name="modulation",
    )(c_all, w_ada, b_ada.reshape(1, n))


def _bias_lookup(rel, tab_ref, col):
    n = jnp.abs(rel)
    large = jnp.full(rel.shape, 8, jnp.int32)
    for t in _BUCKET_STEPS:
        large = large + (n >= t).astype(jnp.int32)
    bucket = jnp.where(n < 8, n, large) + jnp.where(rel > 0, N_BUCKETS // 2, 0)
    val = jnp.zeros(rel.shape, F32)
    for b in range(N_BUCKETS):
        val = jnp.where(bucket == b, tab_ref[b, col], val)
    return val


def _da_bias_kernel(tab_ref, dac_ref, o_ref):
    h = pl.program_id(0)
    sat = _BUCKET_STEPS[-1]
    row = lax.broadcasted_iota(jnp.int32, (LANES, LANES), 0)
    col = lax.broadcasted_iota(jnp.int32, (LANES, LANES), 1)
    far = (jnp.full((LANES, LANES), -dac_ref[h, 1], F32), jnp.full((LANES, LANES), -dac_ref[h, 2], F32))
    for d in range(-2, 3):
        for br in range(T_ATT // LANES):
            for bc in range(T_ATT // LANES):
                base = d * T_ATT + (bc - br) * LANES
                if base + LANES - 1 <= -sat:
                    val = far[0]
                elif base - (LANES - 1) >= sat:
                    val = far[1]
                else:
                    val = LOG2E * _bias_lookup(col - row + base, tab_ref, h) - dac_ref[h, 0]
                o_ref[0, d + 2, br * LANES:(br + 1) * LANES, bc * LANES:(bc + 1) * LANES] = val


def _da_bias_tiles(rel_bias, dac):
    assert T_ATT >= _BUCKET_STEPS[-1]
    return pl.pallas_call(
        _da_bias_kernel,
        grid=(DA_HEADS,),
        in_specs=[pl.BlockSpec(memory_space=pltpu.SMEM), pl.BlockSpec(memory_space=pltpu.SMEM)],
        out_specs=pl.BlockSpec((1, 5, T_ATT, T_ATT), lambda h: (h, 0, 0, 0)),
        out_shape=jax.ShapeDtypeStruct((DA_HEADS, 5, T_ATT, T_ATT), F32),
        compiler_params=_cparams(("arbitrary",), 32),
        name="da_bias_tiles",
    )(rel_bias, dac)


def _dil_bias_kernel(tab_ref, dilc_ref, o_ref):
    g = pl.program_id(0)
    v = pl.program_id(1)
    h = pl.program_id(2)
    row = lax.broadcasted_iota(jnp.int32, (TQ_DIL, W_DIL), 0)
    col = lax.broadcasted_iota(jnp.int32, (TQ_DIL, W_DIL), 1)
    rel = col - row - HALF_DIL * v
    dilation = lax.shift_left(jnp.int32(1), 2 * g)
    val = LOG2E * _bias_lookup(rel * dilation, tab_ref, DA_HEADS + g * DIL_HEADS + h) - dilc_ref[g, h, 0]
    o_ref[0, 0, 0] = jnp.where(jnp.abs(rel) <= HALF_DIL, val, NEG_INF)


def _dil_bias_tiles(rel_bias, dilc):
    return pl.pallas_call(
        _dil_bias_kernel,
        grid=(N_DIL, 3, DIL_HEADS),
        in_specs=[pl.BlockSpec(memory_space=pltpu.SMEM), pl.BlockSpec(memory_space=pltpu.SMEM)],
        out_specs=pl.BlockSpec((1, 1, 1, TQ_DIL, W_DIL), lambda g, v, h: (g, v, h, 0, 0)),
        out_shape=jax.ShapeDtypeStruct((N_DIL, 3, DIL_HEADS, TQ_DIL, W_DIL), F32),
        compiler_params=_cparams(("arbitrary",) * 3, 32),
        name="dil_bias_tiles",
    )(rel_bias, dilc)


_QK_BLOCKS = {0: 0, 1: 1, 2: 2, 3: 3}
for _g in range(N_DIL):
    _QK_BLOCKS[6 + 3 * _g] = 4 + 2 * _g
    _QK_BLOCKS[7 + 3 * _g] = 5 + 2 * _g


def _inproj_kernel(x_ref, mod_ref, g1_ref, w_ref, gain_ref, gmat_ref,
                   qa_ref, ka_ref, va_ref, dil0_ref, dil1_ref, dil2_ref, gate_ref, cls_ref):
    tm = x_ref.shape[1]
    dil_refs = (dil0_ref, dil1_ref, dil2_ref)
    x = x_ref[0]
    mod = mod_ref[0]
    ms = jnp.mean(x * x, axis=-1, keepdims=True)
    h = (x * lax.rsqrt(ms + EPS)) * g1_ref[...]
    h = h * (1.0 + mod[1:2]) + mod[0:1]
    hb = h.astype(BF16)
    gmat = gmat_ref[...]

    for blk in range(N_COL_BLKS):
        acc = jnp.dot(hb, w_ref[:, blk * COL_BLK:(blk + 1) * COL_BLK], preferred_element_type=F32)
        dilation = 1
        if blk < 6:
            dst, off = (qa_ref, ka_ref, va_ref)[blk // 2], (blk % 2) * COL_BLK
        elif blk < 15:
            g = (blk - 6) // 3
            dst, off, dilation = dil_refs[g], ((blk - 6) % 3) * COL_BLK, DIL_CONFIG[g][1]
        else:
            dst, off = gate_ref, (blk - 15) * COL_BLK

        def put(lo, y):
            if dilation == 1:
                dst[0, :, off + lo:off + lo + y.shape[1]] = y.astype(BF16)
            else:
                for k in range(y.shape[1] // LANES):
                    cls_ref[lo // LANES + k] = y[:, k * LANES:(k + 1) * LANES]

        if blk in _QK_BLOCKS:
            gi = _QK_BLOCKS[blk]
            for half in range(COL_BLK // NORM_W):
                lo = half * NORM_W
                a = acc[:, lo:lo + NORM_W]
                hm = jnp.dot((a * a).astype(BF16), gmat, preferred_element_type=F32)
                put(lo, (a * lax.rsqrt(hm + EPS)) * gain_ref[gi:gi + 1, lo:lo + NORM_W])
        elif blk >= 15:
            put(0, 1.0 / (1.0 + jnp.exp(-acc)))
        else:
            put(0, acc)
        if dilation > 1:
            for c in range(dilation):
                rows = [cls_ref[k, pl.ds(c, tm // dilation, stride=dilation), :] for k in range(COL_BLK // LANES)]
                dst[0, c, :, off:off + COL_BLK] = jnp.concatenate(rows, axis=1).astype(BF16)


def _inproj(x, mod, g1, w_in_b, gains, gmat):
    B, S, _ = x.shape
    tm = TM_PROJ
    const = lambda b, i: (0, 0)
    tok = lambda b, i: (b, i, 0)
    dil_shapes = tuple(jax.ShapeDtypeStruct((B, S, 3 * DIL_W) if r == 1 else (B, r, S // r, 3 * DIL_W), BF16)
                       for _, r in DIL_CONFIG)
    dil_specs = tuple(pl.BlockSpec((1, tm, 3 * DIL_W), tok) if r == 1 else
                      pl.BlockSpec((1, r, tm // r, 3 * DIL_W), lambda b, i: (b, 0, i, 0))
                      for _, r in DIL_CONFIG)
    tok_shapes = (jax.ShapeDtypeStruct((B, S, 2 * DA_QK), BF16),
                  jax.ShapeDtypeStruct((B, S, 2 * DA_QK), BF16),
                  jax.ShapeDtypeStruct((B, S, DA_V), BF16))
    gate_shape = jax.ShapeDtypeStruct((B, S, 2 * D_MODEL), BF16)
    tok_spec = lambda s: pl.BlockSpec((1, tm, s.shape[-1]), tok)
    return pl.pallas_call(
        _inproj_kernel,
        grid=(B, S // tm),
        in_specs=[pl.BlockSpec((1, tm, D_MODEL), tok),
                  pl.BlockSpec((1, 6, D_MODEL), lambda b, i: (b, 0, 0)),
                  pl.BlockSpec((1, D_MODEL), const),
                  pl.BlockSpec((D_MODEL, IN_COLS), const, pipeline_mode=pl.Buffered(1)),
                  pl.BlockSpec(gains.shape, const),
                  pl.BlockSpec((NORM_W, NORM_W), const)],
        out_specs=tuple(map(tok_spec, tok_shapes)) + dil_specs + (tok_spec(gate_shape),),
        out_shape=tok_shapes + dil_shapes + (gate_shape,),
        scratch_shapes=[pltpu.VMEM((COL_BLK // LANES, tm, LANES), F32)],
        compiler_params=_cparams(("arbitrary", "arbitrary"), 56),
        name="in_projection",
    )(x, mod, g1, w_in_b, gains, gmat)


def _diffattn_kernel(dac_ref, lam_ref, q_ref, k_ref, v_ref, bias_ref, g_ref, o_ref, vext_ref, acc_ref):
    T = T_ATT
    TQ = q_ref.shape[1]
    n_sub = TQ // T
    h = pl.program_id(1)
    i = pl.program_id(2)
    n_kv = k_ref.shape[1] // T

    @pl.when(i == 0)
    def _():
        vext_ref[:, :DA_VDIM] = v_ref[0]
        vext_ref[:, DA_VDIM:] = jnp.ones((vext_ref.shape[0], DA_VDIM), BF16)

    q = q_ref[0]
    lane = lax.broadcasted_iota(jnp.int32, q.shape, 1)
    zero = jnp.zeros_like(q)
    qs = jnp.concatenate([jnp.where(lane < HEAD_DIM, q, zero), jnp.where(lane >= HEAD_DIM, q, zero)], axis=0)

    def kv_rows(j):
        return pl.ds(pl.multiple_of(j * T, T), T)

    def logits(j):
        s = lax.dot_general(qs, k_ref[0, kv_rows(j), :], (((1,), (1,)), ((), ())),
                            preferred_element_type=F32)
        tiles = [bias_ref[0, jnp.clip(j - (i * n_sub + t), -2, 2) + 2] for t in range(n_sub)]
        return s + jnp.concatenate(tiles + tiles, axis=0)

    fast_ok = dac_ref[h, 3] > 0.5

    @pl.when(fast_ok)
    def _():
        acc_ref[...] = jnp.zeros(acc_ref.shape, F32)

        def body(j, c):
            acc_ref[...] += jnp.dot(jnp.exp2(logits(j)).astype(BF16), vext_ref[kv_rows(j), :],
                                    preferred_element_type=F32)
            return c

        lax.fori_loop(0, n_kv, body, 0, unroll=DA_UNROLL)

    @pl.when(jnp.logical_not(fast_ok))
    def _():
        def body(j, carry):
            m, acc = carry
            u = logits(j)
            m_new = jnp.maximum(m, jnp.max(u, axis=-1, keepdims=True))
            p = jnp.exp2(u - m_new)
            acc = jnp.exp2(m - m_new) * acc + jnp.dot(p.astype(BF16), vext_ref[kv_rows(j), :],
                                                       preferred_element_type=F32)
            return m_new, acc

        init = (jnp.full((2 * TQ, 1), NEG_INF, F32), jnp.zeros(acc_ref.shape, F32))
        acc_ref[...] = lax.fori_loop(0, n_kv, body, init)[1]

    lp = lam_ref[...]
    lam = (jnp.exp(jnp.sum(lp[0:1] * lp[1:2], axis=-1, keepdims=True))
           - jnp.exp(jnp.sum(lp[2:3] * lp[3:4], axis=-1, keepdims=True)) + LAM_INIT)
    o = (acc_ref[:TQ, :DA_VDIM] / acc_ref[:TQ, DA_VDIM:]
         - lam * (acc_ref[TQ:, :DA_VDIM] / acc_ref[TQ:, DA_VDIM:]))
    ms = jnp.mean(o * o, axis=-1, keepdims=True)
    o = (o * lax.rsqrt(ms + EPS)) * g_ref[...] * (1.0 - LAM_INIT)
    o_ref[0] = o.astype(o_ref.dtype)


def _diff_attention(dac, lam_params, qa, ka, va, bias_tiles, subln_g):
    B, S, _ = qa.shape
    T = T_ATT
    TQ = TQ_ATT
    return pl.pallas_call(
        _diffattn_kernel,
        grid=(B, DA_HEADS, S // TQ),
        in_specs=[pl.BlockSpec(memory_space=pltpu.SMEM),
                  pl.BlockSpec((4, HEAD_DIM), lambda b, h, i: (0, 0)),
                  pl.BlockSpec((1, TQ, LANES), lambda b, h, i: (b, i, h)),
                  pl.BlockSpec((1, S, LANES), lambda b, h, i: (b, 0, h)),
                  pl.BlockSpec((1, S, LANES), lambda b, h, i: (b, 0, h)),
                  pl.BlockSpec((1, 5, T, T), lambda b, h, i: (h, 0, 0, 0)),
                  pl.BlockSpec((1, DA_VDIM), lambda b, h, i: (0, 0))],
        out_specs=pl.BlockSpec((1, TQ, LANES), lambda b, h, i: (b, i, h)),
        out_shape=jax.ShapeDtypeStruct((B, S, DA_V), BF16),
        scratch_shapes=[pltpu.VMEM((S, 2 * DA_VDIM), BF16), pltpu.VMEM((2 * TQ, 2 * DA_VDIM), F32)],
        compiler_params=_cparams(("arbitrary",) * 3, 56),
        name="diff_attention",
    )(dac, lam_params, qa, ka, va, bias_tiles, subln_g)


def _dilattn_kernel(g, dilc_ref, q_ref, k_ref, v_ref, bias_ref, o_ref, lse_ref):
    TQ = TQ_DIL
    L = k_ref.shape[2]
    tiles_per_step = q_ref.shape[2] // TQ
    n_q = L // TQ
    low = lax.broadcasted_iota(jnp.int32, (TQ, LANES), 1) < HEAD_DIM
    ones = jnp.ones((W_DIL, LANES), BF16)

    def run(use_max):
        for sub in range(tiles_per_step):
            for hp in range(DIL_HEADS // 2):
                one_tile(use_max, sub, hp)

    def one_tile(use_max, sub, hp):
            i = pl.program_id(2) * tiles_per_step + sub
            w0 = pl.multiple_of(jnp.clip(i * TQ - HALF_DIL, 0, L - W_DIL), HALF_DIL)
            variant = jnp.where(i == 0, 0, jnp.where(i == n_q - 1, 2, 1))
            cs = slice(hp * LANES, (hp + 1) * LANES)
            qp = q_ref[0, 0, sub * TQ:(sub + 1) * TQ, cs]
            kp = k_ref[0, 0, pl.ds(w0, W_DIL), cs]
            vext = jnp.concatenate([v_ref[0, 0, pl.ds(w0, W_DIL), cs], ones], axis=1)
            zero = jnp.zeros_like(qp)
            qs = jnp.concatenate([jnp.where(low, qp, zero), jnp.where(low, zero, qp)], axis=0)
            u = lax.dot_general(qs, kp, (((1,), (1,)), ((), ())), preferred_element_type=F32)
            u = u + jnp.concatenate([bias_ref[0, variant, 2 * hp], bias_ref[0, variant, 2 * hp + 1]], axis=0)
            if use_max:
                m = jnp.max(u, axis=-1, keepdims=True)
                u = u - m
            r = jnp.dot(jnp.exp2(u).astype(BF16), vext, preferred_element_type=F32)
            num, den = r[:, :LANES], r[:, LANES:]
            lse2 = jnp.log2(den)
            if use_max:
                lse2 = lse2 + m
            o_pair = jnp.where(low, num[:TQ] / den[:TQ], num[TQ:] / den[TQ:]).astype(o_ref.dtype)
            lse_pair = LN2 * jnp.where(low, lse2[:TQ] + dilc_ref[g, 2 * hp, 0], lse2[TQ:] + dilc_ref[g, 2 * hp + 1, 0])
            if len(o_ref.shape) == 7:
                n_grp = o_ref.shape[3]
                for t in range(TQ // (n_grp * SUBLANES)):
                    for grp in range(n_grp):
                        lo = (t * n_grp + grp) * SUBLANES
                        o_ref[0, sub * (TQ // (n_grp * SUBLANES)) + t, hp, grp, 0] = o_pair[lo:lo + SUBLANES]
                        lse_ref[0, sub * (TQ // (n_grp * SUBLANES)) + t, hp, grp, 0] = lse_pair[lo:lo + SUBLANES]
            else:
                o_ref[0, 0, sub * TQ:(sub + 1) * TQ, cs] = o_pair
                lse_ref[0, 0, sub * TQ:(sub + 1) * TQ, cs] = lse_pair

    fast_ok = dilc_ref[g, 0, 1] > 0.5
    pl.when(fast_ok)(functools.partial(run, False))
    pl.when(jnp.logical_not(fast_ok))(functools.partial(run, True))


def _dilated_attention(dilc, dil, bias_tiles, g):
    B, r, L, _ = dil.shape
    kv_mode = pl.Buffered(1) if L * DIL_W * 2 > (4 << 20) else pl.Buffered(2)
    qmap = lambda b, c, i: (b, c, i, 0)
    rows = min(ROWS_DIL, L)
    if r == 1:
        o_spec = pl.BlockSpec((1, 1, rows, DIL_W), qmap)
        o_shape, o_dtype = (B, r, L, DIL_W), BF16
    else:
        n = TM_OUT // r
        nk = DIL_W // LANES
        o_spec = pl.BlockSpec((1, rows // n, nk, n // SUBLANES, 1, SUBLANES, LANES),
                              lambda b, c, i: (b, i, 0, 0, c, 0, 0))
        o_shape, o_dtype = (B, L // n, nk, n // SUBLANES, r, SUBLANES, LANES), F32
    return pl.pallas_call(
        functools.partial(_dilattn_kernel, g),
        grid=(B, r, L // rows),
        in_specs=[pl.BlockSpec(memory_space=pltpu.SMEM),
                  pl.BlockSpec((1, 1, rows, DIL_W), qmap),
                  pl.BlockSpec((1, 1, L, DIL_W), lambda b, c, i: (b, c, 0, 1), pipeline_mode=kv_mode),
                  pl.BlockSpec((1, 1, L, DIL_W), lambda b, c, i: (b, c, 0, 2), pipeline_mode=kv_mode),
                  pl.BlockSpec((1, 3, DIL_HEADS, TQ_DIL, W_DIL), lambda b, c, i: (g, 0, 0, 0, 0))],
        out_specs=(o_spec, o_spec),
        out_shape=(jax.ShapeDtypeStruct(o_shape, o_dtype), jax.ShapeDtypeStruct(o_shape, F32)),
        compiler_params=_cparams(("arbitrary",) * 3, 48),
        name=f"dilated_attention_r{r}",
    )(dilc, dil, dil, dil, bias_tiles)


def _interleave_classes(ref, r, k, il_ref):
    nk = DIL_W // LANES
    n = ref.shape[2] // (r * nk)
    pieces = [ref[0, 0, pl.ds(k * n * r + (l // SUBLANES) * r * SUBLANES + l % SUBLANES, r, stride=SUBLANES), :]
              for l in range(n)]
    if r % 8 == 0:
        return jnp.concatenate(pieces, axis=0)
    for l in range(n):
        il_ref[pl.ds(l * r, r), :] = pieces[l]
    return il_ref[...]


def _merge_mlp_kernel(x_ref, mod_ref, ya_ref, o0_ref, l0_ref, o1_ref, l1_ref, o2_ref, l2_ref, gate_ref,
                      wa_ref, wb_ref, wo_ref, g2_ref, wu_ref, wd_ref, out_ref, il_ref, yb_ref):
    mod = mod_ref[0]

    @pl.when(pl.program_id(1) == 0)
    def _():
        yb_ref[...] = jnp.zeros(yb_ref.shape, BF16)

    yb = yb_ref[...]
    r1, r2 = DIL_CONFIG[1][1], DIL_CONFIG[2][1]

    def combine_chunk(k):
        cs = slice(k * LANES, (k + 1) * LANES)
        outs = (o0_ref[0, :, cs].astype(F32), _interleave_classes(o1_ref, r1, k, il_ref.at[0, k]),
                _interleave_classes(o2_ref, r2, k, il_ref.at[0, k]))
        lses = (l0_ref[0, :, cs], _interleave_classes(l1_ref, r1, k, il_ref.at[1, k]),
                _interleave_classes(l2_ref, r2, k, il_ref.at[1, k]))
        mx = jnp.maximum(jnp.maximum(lses[0], lses[1]), lses[2])
        es = [jnp.exp(l - mx) for l in lses]
        den = es[0] + es[1] + es[2]
        yb_ref[:, cs] = ((es[0] * outs[0] + es[1] * outs[1] + es[2] * outs[2]) / den).astype(BF16)

    pa = jnp.dot(ya_ref[0], wa_ref[...], preferred_element_type=F32)
    pb = jnp.dot(yb, wb_ref[...], preferred_element_type=F32)
    gates = gate_ref[0]
    merged = gates[:, :D_MODEL].astype(F32) * pa + gates[:, D_MODEL:].astype(F32) * pb
    x = x_ref[0] + mod[2:3] * jnp.dot(merged.astype(BF16), wo_ref[...], preferred_element_type=F32)

    ms = jnp.mean(x * x, axis=-1, keepdims=True)
    h = (x * lax.rsqrt(ms + EPS)) * g2_ref[...]
    hb = (h * (1.0 + mod[4:5]) + mod[3:4]).astype(BF16)
    acc = jnp.zeros(x.shape, F32)
    for f in range(D_FF // D_MODEL):
        cs = slice(f * D_MODEL, (f + 1) * D_MODEL)
        u = jnp.maximum(jnp.dot(hb, wu_ref[:, cs], preferred_element_type=F32), 0.0)
        combine_chunk(f)
        acc = acc + jnp.dot((u * u).astype(BF16), wd_ref[cs, :], preferred_element_type=F32)
    out_ref[0] = x + mod[5:6] * acc


def _merge_mlp(x, mod, ya, o0, l0, o1, l1, o2, l2, gates, wa, wb, wo, g2, wu, wd):
    B, S, _ = x.shape
    tm = TM_OUT
    n = S // tm
    const = lambda b, i: (0, 0)
    tspec = lambda w: pl.BlockSpec((1, tm, w), lambda b, i: (b, jnp.maximum(i - 1, 0), 0))
    dspec = lambda w: pl.BlockSpec((1, tm, w), lambda b, i: (b, jnp.minimum(i, n - 1), 0))
    cspec = lambda a: pl.BlockSpec((1, 1) + a.shape[2:], lambda b, i: (b, jnp.minimum(i, n - 1), 0, 0))
    wspec = lambda a: pl.BlockSpec(a.shape, const, pipeline_mode=pl.Buffered(1))
    return pl.pallas_call(
        _merge_mlp_kernel,
        grid=(B, n + 1),
        in_specs=[tspec(D_MODEL), pl.BlockSpec((1, 6, D_MODEL), lambda b, i: (b, 0, 0)), tspec(DA_V),
                  dspec(DIL_W), dspec(DIL_W), cspec(o1), cspec(l1), cspec(o2), cspec(l2),
                  tspec(2 * D_MODEL), wspec(wa), wspec(wb), wspec(wo),
                  pl.BlockSpec((1, D_MODEL), const), wspec(wu), wspec(wd)],
        out_specs=tspec(D_MODEL),
        out_shape=jax.ShapeDtypeStruct((B, S, D_MODEL), F32),
        scratch_shapes=[pltpu.VMEM((2, DIL_W // LANES, tm, LANES), F32), pltpu.VMEM((tm, DIL_W), BF16)],
        compiler_params=_cparams(("arbitrary", "arbitrary"), 56),
        name="merge_mlp",
    )(x, mod, ya, o0, l0, o1, l1, o2, l2, gates, wa, wb, wo, g2, wu, wd)


def _pair_heads(a, b):
    d = a.shape[0]
    return jnp.stack([a.reshape(d, DA_HEADS, HEAD_DIM), b.reshape(d, DA_HEADS, HEAD_DIM)],
                     axis=2).reshape(d, 2 * DA_QK)


def _layer(x, mod, p):
    B, S, _ = x.shape
    qa, ka, va, dil0, dil1, dil2, gates = _inproj(x, mod, p["g1"], p["w_in"], p["gains"], p["gmat"])
    ya = _diff_attention(p["dac"], p["lam"], qa, ka, va, p["da_bias"], p["subln_g"])
    o0, l0 = _dilated_attention(p["dilc"], dil0.reshape(B, 1, S, 3 * DIL_W), p["dil_bias"], 0)
    o1, l1 = _dilated_attention(p["dilc"], dil1, p["dil_bias"], 1)
    o2, l2 = _dilated_attention(p["dilc"], dil2, p["dil_bias"], 2)
    rows = lambda a: a.reshape(B, a.shape[1], -1, LANES)
    return _merge_mlp(x, mod, ya, o0.reshape(B, S, DIL_W), l0.reshape(B, S, DIL_W),
                      rows(o1), rows(l1), rows(o2), rows(l2), gates,
                      p["w_br_a"], p["w_br_b"], p["w_o"], p["g2"], p["w_up"], p["w_down"])


def kernel(x_prompt, x_sample, c_prompt, c_sample, rel_bias, norm1_g, w_ada, b_ada, w_in, qn_a, kn_a,
           lambda_q1, lambda_k1, lambda_q2, lambda_k2, subln_g, qn_b, kn_b, w_br_a, w_br_b, w_o,
           norm2_g, w_up, w_down):
    nbp, nbs = c_prompt.shape[0], c_sample.shape[0]
    pad = (-(nbp + nbs)) % 8
    c_all = jnp.concatenate([c_prompt, c_sample, jnp.zeros((pad, D_MODEL), F32)], axis=0)
    mod = _modulation(c_all, w_ada[0], b_ada[0]).reshape(-1, 6, D_MODEL)

    w = w_in[0]
    q1, q2, k1, k2 = (w[:, n * DA_QK:(n + 1) * DA_QK] for n in range(4))
    w_perm = jnp.concatenate([_pair_heads(q1, q2), _pair_heads(k1, k2), w[:, 4 * DA_QK:]], axis=1)

    scale = HEAD_DIM ** -0.5
    tile8 = lambda v: jnp.tile(v, DIL_W // HEAD_DIM)
    gain_rows = [tile8(qn_a[0]) * (scale * LOG2E)] * 2 + [tile8(kn_a[0])] * 2
    for g in range(N_DIL):
        gain_rows += [tile8(qn_b[0, g]) * (scale * LOG2E), tile8(kn_b[0, g])]
    ids = jnp.arange(NORM_W) // HEAD_DIM
    gmat = jnp.where(ids[:, None] == ids[None, :], 1.0 / HEAD_DIM, 0.0).astype(BF16)

    def logit_reference(q_gain, k_gain, tab):
        qk_bound = HEAD_DIM * scale * jnp.max(jnp.abs(q_gain)) * jnp.max(jnp.abs(k_gain)) * BF16_ROUND_MARGIN
        c_ref = LOG2E * (qk_bound + jnp.max(tab, axis=0)) - DA_HEADROOM
        u_min = LOG2E * (jnp.min(tab, axis=0) - qk_bound) - c_ref
        return c_ref, jnp.all(u_min >= DA_MIN_EXP).astype(F32) * jnp.ones_like(c_ref)

    da_tab = rel_bias[:, :DA_HEADS]
    c_ref, da_ok = logit_reference(qn_a[0], kn_a[0], da_tab)
    dac = jnp.stack([c_ref, c_ref - LOG2E * da_tab[N_BUCKETS // 2 - 1], c_ref - LOG2E * da_tab[N_BUCKETS - 1],
                     da_ok], axis=1)
    dilc = jnp.stack([jnp.stack(logit_reference(
        qn_b[0, g], kn_b[0, g], rel_bias[:, DA_HEADS + g * DIL_HEADS:DA_HEADS + (g + 1) * DIL_HEADS]), axis=1)
        for g in range(N_DIL)], axis=0)

    p = {
        "g1": norm1_g[0].reshape(1, D_MODEL),
        "g2": norm2_g[0].reshape(1, D_MODEL),
        "w_in": w_perm.astype(BF16),
        "gains": jnp.stack(gain_rows, axis=0),
        "gmat": gmat,
        "lam": jnp.stack([lambda_q1[0], lambda_k1[0], lambda_q2[0], lambda_k2[0]], axis=0),
        "subln_g": subln_g[0].reshape(1, DA_VDIM),
        "dac": dac,
        "da_bias": _da_bias_tiles(rel_bias, dac),
        "dilc": dilc,
        "dil_bias": _dil_bias_tiles(rel_bias, dilc),
        "w_br_a": w_br_a[0].astype(BF16),
        "w_br_b": w_br_b[0].astype(BF16),
        "w_o": w_o[0].astype(BF16),
        "w_up": w_up[0].astype(BF16),
        "w_down": w_down[0].astype(BF16),
    }
    y_prompt = _layer(x_prompt, mod[:nbp], p)
    y_sample = _layer(x_sample, mod[nbp:nbp + nbs], p)
    return (y_prompt, y_sample)
```

```python
import functools
import math

import jax
import jax.numpy as jnp
from jax import lax
from jax.experimental import pallas as pl
from jax.experimental.pallas import tpu as pltpu

F32 = jnp.float32
BF16 = jnp.bfloat16

D_MODEL = 1024
HEAD_DIM = 64
DA_HEADS = 8
DA_VDIM = 2 * HEAD_DIM
DIL_CONFIG = ((128, 1), (512, 4), (2048, 16))
N_DIL = 3
DIL_HEADS = 8
D_FF = 4 * D_MODEL
N_BUCKETS = 32
NEG_INF = -1e30
EPS = 1e-6
LAM_INIT = 0.8 - 0.6 * math.exp(-0.3 * 0)

DA_QK = DA_HEADS * HEAD_DIM
DA_V = DA_HEADS * DA_VDIM
DIL_W = DIL_HEADS * HEAD_DIM
DIL_COLS = 3 * N_DIL * DIL_W
IN_COLS = 4 * DA_QK + DA_V + DIL_COLS + 2 * D_MODEL

LANES = 128
SUBLANES = 8
COL_BLK = 512
N_COL_BLKS = IN_COLS // COL_BLK
NORM_W = 256

TM_PROJ = 512
TM_OUT = 256
T_ATT = 512
TQ_ATT = 1024
TQ_DIL = 128
ROWS_DIL = 512
HALF_DIL = 64
W_DIL = TQ_DIL + 2 * HALF_DIL

LOG2E = math.log2(math.e)
LN2 = math.log(2.0)
DA_HEADROOM = 64.0
DA_MIN_EXP = -60.0
DA_UNROLL = 8
BF16_ROUND_MARGIN = 1.02

_BUCKET_STEPS = (12, 16, 23, 32, 46, 64, 91)


def _cparams(sem, vmem_mb):
    return pltpu.CompilerParams(dimension_semantics=sem, vmem_limit_bytes=vmem_mb * 1024 * 1024)


def _mod_kernel(c_ref, w_ref, b_ref, o_ref):
    c = c_ref[...]
    a = c * (1.0 / (1.0 + jnp.exp(-c)))
    o_ref[...] = jnp.dot(a, w_ref[...], preferred_element_type=F32,
                         precision=lax.Precision.HIGHEST) + b_ref[...]


def _modulation(c_all, w_ada, b_ada):
    rows = c_all.shape[0]
    n = w_ada.shape[1]
    return pl.pallas_call(
        _mod_kernel,
        grid=(n // D_MODEL,),
        in_specs=[pl.BlockSpec((rows, D_MODEL), lambda j: (0, 0)),
                  pl.BlockSpec((D_MODEL, D_MODEL), lambda j: (0, j)),
                  pl.BlockSpec((1, D_MODEL), lambda j: (0, j))],
        out_specs=pl.BlockSpec((rows, D_MODEL), lambda j: (0, j)),
        out_shape=jax.ShapeDtypeStruct((rows, n), F32),
        compiler_params=_cparams(("arbitrary",), 32),
        name="modulation",
    )(c_all, w_ada, b_ada.reshape(1, n))


def _bias_lookup(rel, tab_ref, col):
    n = jnp.abs(rel)
    large = jnp.full(rel.shape, 8, jnp.int32)
    for t in _BUCKET_STEPS:
        large = large + (n >= t).astype(jnp.int32)
    bucket = jnp.where(n < 8, n, large) + jnp.where(rel > 0, N_BUCKETS // 2, 0)
    val = jnp.zeros(rel.shape, F32)
    for b in range(N_BUCKETS):
        val = jnp.where(bucket == b, tab_ref[b, col], val)
    return val


def _da_bias_kernel(tab_ref, dac_ref, o_ref):
    h = pl.program_id(0)
    sat = _BUCKET_STEPS[-1]
    row = lax.broadcasted_iota(jnp.int32, (LANES, LANES), 0)
    col = lax.broadcasted_iota(jnp.int32, (LANES, LANES), 1)
    far = (jnp.full((LANES, LANES), -dac_ref[h, 1], F32), jnp.full((LANES, LANES), -dac_ref[h, 2], F32))
    for d in range(-2, 3):
        for br in range(T_ATT // LANES):
            for bc in range(T_ATT // LANES):
                base = d * T_ATT + (bc - br) * LANES
                if base + LANES - 1 <= -sat:
                    val = far[0]
                elif base - (LANES - 1) >= sat:
                    val = far[1]
                else:
                    val = LOG2E * _bias_lookup(col - row + base, tab_ref, h) - dac_ref[h, 0]
                o_ref[0, d + 2, br * LANES:(br + 1) * LANES, bc * LANES:(bc + 1) * LANES] = val


def _da_bias_tiles(rel_bias, dac):
    assert T_ATT >= _BUCKET_STEPS[-1]
    return pl.pallas_call(
        _da_bias_kernel,
        grid=(DA_HEADS,),
        in_specs=[pl.BlockSpec(memory_space=pltpu.SMEM), pl.BlockSpec(memory_space=pltpu.SMEM)],
        out_specs=pl.BlockSpec((1, 5, T_ATT, T_ATT), lambda h: (h, 0, 0, 0)),
        out_shape=jax.ShapeDtypeStruct((DA_HEADS, 5, T_ATT, T_ATT), F32),
        compiler_params=_cparams(("arbitrary",), 32),
        name="da_bias_tiles",
    )(rel_bias, dac)


def _dil_bias_kernel(tab_ref, dilc_ref, o_ref):
    g = pl.program_id(0)
    v = pl.program_id(1)
    h = pl.program_id(2)
    row = lax.broadcasted_iota(jnp.int32, (TQ_DIL, W_DIL), 0)
    col = lax.broadcasted_iota(jnp.int32, (TQ_DIL, W_DIL), 1)
    rel = col - row - HALF_DIL * v
    dilation = lax.shift_left(jnp.int32(1), 2 * g)
    val = LOG2E * _bias_lookup(rel * dilation, tab_ref, DA_HEADS + g * DIL_HEADS + h) - dilc_ref[g, h, 0]
    o_ref[0, 0, 0] = jnp.where(jnp.abs(rel) <= HALF_DIL, val, NEG_INF)


def _dil_bias_tiles(rel_bias, dilc):
    return pl.pallas_call(
        _dil_bias_kernel,
        grid=(N_DIL, 3, DIL_HEADS),
        in_specs=[pl.BlockSpec(memory_space=pltpu.SMEM), pl.BlockSpec(memory_space=pltpu.SMEM)],
        out_specs=pl.BlockSpec((1, 1, 1, TQ_DIL, W_DIL), lambda g, v, h: (g, v, h, 0, 0)),
        out_shape=jax.ShapeDtypeStruct((N_DIL, 3, DIL_HEADS, TQ_DIL, W_DIL), F32),
        compiler_params=_cparams(("arbitrary",) * 3, 32),
        name="dil_bias_tiles",
    )(rel_bias, dilc)


_QK_BLOCKS = {0: 0, 1: 1, 2: 2, 3: 3}
for _g in range(N_DIL):
    _QK_BLOCKS[6 + 3 * _g] = 4 + 2 * _g
    _QK_BLOCKS[7 + 3 * _g] = 5 + 2 * _g


def _inproj_kernel(x_ref, mod_ref, g1_ref, w_ref, gain_ref, gmat_ref,
                   qa_ref, ka_ref, va_ref, dil0_ref, dil1_ref, dil2_ref, gate_ref, cls_ref):
    tm = x_ref.shape[1]
    dil_refs = (dil0_ref, dil1_ref, dil2_ref)
    x = x_ref[0]
    mod = mod_ref[0]
    ms = jnp.mean(x * x, axis=-1, keepdims=True)
    h = (x * lax.rsqrt(ms + EPS)) * g1_ref[...]
    h = h * (1.0 + mod[1:2]) + mod[0:1]
    gmat = gmat_ref[...]

    for k in range(D_MODEL // LANES):
        cls_ref[k] = h[:, k * LANES:(k + 1) * LANES]

    def class_major(r):
        chunks = [jnp.concatenate([cls_ref[k, pl.ds(c, tm // r, stride=r), :] for c in range(r)], axis=0)
                  for k in range(D_MODEL // LANES)]
        return jnp.concatenate(chunks, axis=1).astype(BF16)

    hb_by_dilation = {1: h.astype(BF16)}
    for _, r in DIL_CONFIG:
        if r not in hb_by_dilation:
            hb_by_dilation[r] = class_major(r)

    order = sorted(range(N_COL_BLKS), key=lambda b: (0 if b >= 15 else 1 if b in _QK_BLOCKS else 2, b))
    for blk in order:
        dilation = 1
        if blk < 6:
            dst, off = (qa_ref, ka_ref, va_ref)[blk // 2], (blk % 2) * COL_BLK
        elif blk < 15:
            g = (blk - 6) // 3
            dst, off, dilation = dil_refs[g], ((blk - 6) % 3) * COL_BLK, DIL_CONFIG[g][1]
        else:
            dst, off = gate_ref, (blk - 15) * COL_BLK
        acc = jnp.dot(hb_by_dilation[dilation], w_ref[:, blk * COL_BLK:(blk + 1) * COL_BLK],
                      preferred_element_type=F32)

        def put(lo, y):
            cols = slice(off + lo, off + lo + y.shape[1])
            if dilation == 1:
                dst[0, :, cols] = y.astype(BF16)
            else:
                n = tm // dilation
                for c in range(dilation):
                    dst[0, c, :, cols] = y[c * n:(c + 1) * n].astype(BF16)

        if blk in _QK_BLOCKS:
            gi = _QK_BLOCKS[blk]
            for half in range(COL_BLK // NORM_W):
                lo = half * NORM_W
                a = acc[:, lo:lo + NORM_W]
                hm = jnp.dot((a * a).astype(BF16), gmat, preferred_element_type=F32)
                put(lo, (a * lax.rsqrt(hm + EPS)) * gain_ref[gi:gi + 1, lo:lo + NORM_W])
        elif blk >= 15:
            put(0, 1.0 / (1.0 + jnp.exp(-acc)))
        else:
            put(0, acc)


def _inproj(x, mod, g1, w_in_b, gains, gmat):
    B, S, _ = x.shape
    tm = TM_PROJ
    const = lambda b, i: (0, 0)
    tok = lambda b, i: (b, i, 0)
    dil_shapes = tuple(jax.ShapeDtypeStruct((B, S, 3 * DIL_W) if r == 1 else (B, r, S // r, 3 * DIL_W), BF16)
                       for _, r in DIL_CONFIG)
    dil_specs = tuple(pl.BlockSpec((1, tm, 3 * DIL_W), tok) if r == 1 else
                      pl.BlockSpec((1, r, tm // r, 3 * DIL_W), lambda b, i: (b, 0, i, 0))
                      for _, r in DIL_CONFIG)
    tok_shapes = (jax.ShapeDtypeStruct((B, S, 2 * DA_QK), BF16),
                  jax.ShapeDtypeStruct((B, S, 2 * DA_QK), BF16),
                  jax.ShapeDtypeStruct((B, S, DA_V), BF16))
    gate_shape = jax.ShapeDtypeStruct((B, S, 2 * D_MODEL), BF16)
    tok_spec = lambda s: pl.BlockSpec((1, tm, s.shape[-1]), tok)
    return pl.pallas_call(
        _inproj_kernel,
        grid=(B, S // tm),
        in_specs=[pl.BlockSpec((1, tm, D_MODEL), tok),
                  pl.BlockSpec((1, 6, D_MODEL), lambda b, i: (b, 0, 0)),
                  pl.BlockSpec((1, D_MODEL), const),
                  pl.BlockSpec((D_MODEL, IN_COLS), const, pipeline_mode=pl.Buffered(1)),
                  pl.BlockSpec(gains.shape, const),
                  pl.BlockSpec((NORM_W, NORM_W), const)],
        out_specs=tuple(map(tok_spec, tok_shapes)) + dil_specs + (tok_spec(gate_shape),),
        out_shape=tok_shapes + dil_shapes + (gate_shape,),
        scratch_shapes=[pltpu.VMEM((D_MODEL // LANES, tm, LANES), F32)],
        compiler_params=_cparams(("arbitrary", "arbitrary"), 56),
        name="in_projection",
    )(x, mod, g1, w_in_b, gains, gmat)


def _diffattn_kernel(dac_ref, lam_ref, q_ref, k_ref, v_ref, bias_ref, g_ref, o_ref, vext_ref, acc_ref):
    T = T_ATT
    TQ = q_ref.shape[1]
    n_sub = TQ // T
    h = pl.program_id(1)
    i = pl.program_id(2)
    n_kv = k_ref.shape[1] // T

    @pl.when(i == 0)
    def _():
        vext_ref[:, :DA_VDIM] = v_ref[0]
        vext_ref[:, DA_VDIM:] = jnp.ones((vext_ref.shape[0], DA_VDIM), BF16)

    q = q_ref[0]
    lane = lax.broadcasted_iota(jnp.int32, q.shape, 1)
    zero = jnp.zeros_like(q)
    qs = jnp.concatenate([jnp.where(lane < HEAD_DIM, q, zero), jnp.where(lane >= HEAD_DIM, q, zero)], axis=0)

    def kv_rows(j):
        return pl.ds(pl.multiple_of(j * T, T), T)

    def logits(j):
        s = lax.dot_general(qs, k_ref[0, kv_rows(j), :], (((1,), (1,)), ((), ())),
                            preferred_element_type=F32)
        tiles = [bias_ref[0, jnp.clip(j - (i * n_sub + t), -2, 2) + 2] for t in range(n_sub)]
        return s + jnp.concatenate(tiles + tiles, axis=0)

    fast_ok = dac_ref[h, 3] > 0.5

    @pl.when(fast_ok)
    def _():
        acc_ref[...] = jnp.zeros(acc_ref.shape, F32)

        def body(j, c):
            acc_ref[...] += jnp.dot(jnp.exp2(logits(j)).astype(BF16), vext_ref[kv_rows(j), :],
                                    preferred_element_type=F32)
            return c

        lax.fori_loop(0, n_kv, body, 0, unroll=DA_UNROLL)

    @pl.when(jnp.logical_not(fast_ok))
    def _():
        def body(j, carry):
            m, acc = carry
            u = logits(j)
            m_new = jnp.maximum(m, jnp.max(u, axis=-1, keepdims=True))
            p = jnp.exp2(u - m_new)
            acc = jnp.exp2(m - m_new) * acc + jnp.dot(p.astype(BF16), vext_ref[kv_rows(j), :],
                                                       preferred_element_type=F32)
            return m_new, acc

        init = (jnp.full((2 * TQ, 1), NEG_INF, F32), jnp.zeros(acc_ref.shape, F32))
        acc_ref[...] = lax.fori_loop(0, n_kv, body, init)[1]

    lp = lam_ref[...]
    lam = (jnp.exp(jnp.sum(lp[0:1] * lp[1:2], axis=-1, keepdims=True))
           - jnp.exp(jnp.sum(lp[2:3] * lp[3:4], axis=-1, keepdims=True)) + LAM_INIT)
    o = (acc_ref[:TQ, :DA_VDIM] / acc_ref[:TQ, DA_VDIM:]
         - lam * (acc_ref[TQ:, :DA_VDIM] / acc_ref[TQ:, DA_VDIM:]))
    ms = jnp.mean(o * o, axis=-1, keepdims=True)
    o = (o * lax.rsqrt(ms + EPS)) * g_ref[...] * (1.0 - LAM_INIT)
    o_ref[0] = o.astype(o_ref.dtype)


def _diff_attention(dac, lam_params, qa, ka, va, bias_tiles, subln_g):
    B, S, _ = qa.shape
    T = T_ATT
    TQ = TQ_ATT
    return pl.pallas_call(
        _diffattn_kernel,
        grid=(B, DA_HEADS, S // TQ),
        in_specs=[pl.BlockSpec(memory_space=pltpu.SMEM),
                  pl.BlockSpec((4, HEAD_DIM), lambda b, h, i: (0, 0)),
                  pl.BlockSpec((1, TQ, LANES), lambda b, h, i: (b, i, h)),
                  pl.BlockSpec((1, S, LANES), lambda b, h, i: (b, 0, h)),
                  pl.BlockSpec((1, S, LANES), lambda b, h, i: (b, 0, h)),
                  pl.BlockSpec((1, 5, T, T), lambda b, h, i: (h, 0, 0, 0)),
                  pl.BlockSpec((1, DA_VDIM), lambda b, h, i: (0, 0))],
        out_specs=pl.BlockSpec((1, TQ, LANES), lambda b, h, i: (b, i, h)),
        out_shape=jax.ShapeDtypeStruct((B, S, DA_V), BF16),
        scratch_shapes=[pltpu.VMEM((S, 2 * DA_VDIM), BF16), pltpu.VMEM((2 * TQ, 2 * DA_VDIM), F32)],
        compiler_params=_cparams(("arbitrary",) * 3, 56),
        name="diff_attention",
    )(dac, lam_params, qa, ka, va, bias_tiles, subln_g)


def _dilattn_kernel(g, dilc_ref, q_ref, k_ref, v_ref, bias_ref, o_ref, lse_ref):
    TQ = TQ_DIL
    L = k_ref.shape[2]
    tiles_per_step = q_ref.shape[2] // TQ
    n_q = L // TQ
    low = lax.broadcasted_iota(jnp.int32, (TQ, LANES), 1) < HEAD_DIM
    ones = jnp.ones((W_DIL, LANES), BF16)

    def run(use_max):
        for sub in range(tiles_per_step):
            for hp in range(DIL_HEADS // 2):
                one_tile(use_max, sub, hp)

    def one_tile(use_max, sub, hp):
            i = pl.program_id(2) * tiles_per_step + sub
            w0 = pl.multiple_of(jnp.clip(i * TQ - HALF_DIL, 0, L - W_DIL), HALF_DIL)
            variant = jnp.where(i == 0, 0, jnp.where(i == n_q - 1, 2, 1))
            cs = slice(hp * LANES, (hp + 1) * LANES)
            qp = q_ref[0, 0, sub * TQ:(sub + 1) * TQ, cs]
            kp = k_ref[0, 0, pl.ds(w0, W_DIL), cs]
            vext = jnp.concatenate([v_ref[0, 0, pl.ds(w0, W_DIL), cs], ones], axis=1)
            zero = jnp.zeros_like(qp)
            qs = jnp.concatenate([jnp.where(low, qp, zero), jnp.where(low, zero, qp)], axis=0)
            u = lax.dot_general(qs, kp, (((1,), (1,)), ((), ())), preferred_element_type=F32)
            u = u + jnp.concatenate([bias_ref[0, variant, 2 * hp], bias_ref[0, variant, 2 * hp + 1]], axis=0)
            if use_max:
                m = jnp.max(u, axis=-1, keepdims=True)
                u = u - m
            r = jnp.dot(jnp.exp2(u).astype(BF16), vext, preferred_element_type=F32)
            num, den = r[:, :LANES], r[:, LANES:]
            lse2 = jnp.log2(den)
            if use_max:
                lse2 = lse2 + m
            o_pair = jnp.where(low, num[:TQ] / den[:TQ], num[TQ:] / den[TQ:]).astype(o_ref.dtype)
            lse_pair = LN2 * jnp.where(low, lse2[:TQ] + dilc_ref[g, 2 * hp, 0], lse2[TQ:] + dilc_ref[g, 2 * hp + 1, 0])
            if len(o_ref.shape) == 7:
                n_grp = o_ref.shape[3]
                for t in range(TQ // (n_grp * SUBLANES)):
                    for grp in range(n_grp):
                        lo = (t * n_grp + grp) * SUBLANES
                        o_ref[0, sub * (TQ // (n_grp * SUBLANES)) + t, hp, grp, 0] = o_pair[lo:lo + SUBLANES]
                        lse_ref[0, sub * (TQ // (n_grp * SUBLANES)) + t, hp, grp, 0] = lse_pair[lo:lo + SUBLANES]
            else:
                o_ref[0, 0, sub * TQ:(sub + 1) * TQ, cs] = o_pair
                lse_ref[0, 0, sub * TQ:(sub + 1) * TQ, cs] = lse_pair

    fast_ok = dilc_ref[g, 0, 1] > 0.5
    pl.when(fast_ok)(functools.partial(run, False))
    pl.when(jnp.logical_not(fast_ok))(functools.partial(run, True))


def _dilated_attention(dilc, dil, bias_tiles, g):
    B, r, L, _ = dil.shape
    kv_mode = pl.Buffered(1) if L * DIL_W * 2 > (4 << 20) else pl.Buffered(2)
    qmap = lambda b, c, i: (b, c, i, 0)
    rows = min(ROWS_DIL, L)
    if r == 1:
        o_spec = pl.BlockSpec((1, 1, rows, DIL_W), qmap)
        o_shape, o_dtype = (B, r, L, DIL_W), BF16
    else:
        n = TM_OUT // r
        nk = DIL_W // LANES
        o_spec = pl.BlockSpec((1, rows // n, nk, n // SUBLANES, 1, SUBLANES, LANES),
                              lambda b, c, i: (b, i, 0, 0, c, 0, 0))
        o_shape, o_dtype = (B, L // n, nk, n // SUBLANES, r, SUBLANES, LANES), F32
    return pl.pallas_call(
        functools.partial(_dilattn_kernel, g),
        grid=(B, r, L // rows),
        in_specs=[pl.BlockSpec(memory_space=pltpu.SMEM),
                  pl.BlockSpec((1, 1, rows, DIL_W), qmap),
                  pl.BlockSpec((1, 1, L, DIL_W), lambda b, c, i: (b, c, 0, 1), pipeline_mode=kv_mode),
                  pl.BlockSpec((1, 1, L, DIL_W), lambda b, c, i: (b, c, 0, 2), pipeline_mode=kv_mode),
                  pl.BlockSpec((1, 3, DIL_HEADS, TQ_DIL, W_DIL), lambda b, c, i: (g, 0, 0, 0, 0))],
        out_specs=(o_spec, o_spec),
        out_shape=(jax.ShapeDtypeStruct(o_shape, o_dtype), jax.ShapeDtypeStruct(o_shape, F32)),
        compiler_params=_cparams(("arbitrary",) * 3, 48),
        name=f"dilated_attention_r{r}",
    )(dilc, dil, dil, dil, bias_tiles)


def _interleave_classes(ref, r, k, il_ref):
    nk = DIL_W // LANES
    n = ref.shape[2] // (r * nk)
    pieces = [ref[0, 0, pl.ds(k * n * r + (l // SUBLANES) * r * SUBLANES + l % SUBLANES, r, stride=SUBLANES), :]
              for l in range(n)]
    if r % 8 == 0:
        return jnp.concatenate(pieces, axis=0)
    for l in range(n):
        il_ref[pl.ds(l * r, r), :] = pieces[l]
    return il_ref[...]


def _merge_mlp_kernel(x_ref, mod_ref, ya_ref, o0_ref, l0_ref, o1_ref, l1_ref, o2_ref, l2_ref, gate_ref,
                      wa_ref, wb_ref, wo_ref, g2_ref, wu_ref, wd_ref, out_ref, il_ref, yb_ref):
    mod = mod_ref[0]

    @pl.when(pl.program_id(1) == 0)
    def _():
        yb_ref[...] = jnp.zeros(yb_ref.shape, BF16)

    yb = yb_ref[...]
    r1, r2 = DIL_CONFIG[1][1], DIL_CONFIG[2][1]

    def combine_chunk(k):
        cs = slice(k * LANES, (k + 1) * LANES)
        outs = (o0_ref[0, :, cs].astype(F32), _interleave_classes(o1_ref, r1, k, il_ref.at[0, k]),
                _interleave_classes(o2_ref, r2, k, il_ref.at[0, k]))
        lses = (l0_ref[0, :, cs], _interleave_classes(l1_ref, r1, k, il_ref.at[1, k]),
                _interleave_classes(l2_ref, r2, k, il_ref.at[1, k]))
        mx = jnp.maximum(jnp.maximum(lses[0], lses[1]), lses[2])
        es = [jnp.exp(l - mx) for l in lses]
        den = es[0] + es[1] + es[2]
        yb_ref[:, cs] = ((es[0] * outs[0] + es[1] * outs[1] + es[2] * outs[2]) / den).astype(BF16)

    pa = jnp.dot(ya_ref[0], wa_ref[...], preferred_element_type=F32)
    pb = jnp.dot(yb, wb_ref[...], preferred_element_type=F32)
    gates = gate_ref[0]
    merged = gates[:, :D_MODEL].astype(F32) * pa + gates[:, D_MODEL:].astype(F32) * pb
    x = x_ref[0] + mod[2:3] * jnp.dot(merged.astype(BF16), wo_ref[...], preferred_element_type=F32)

    ms = jnp.mean(x * x, axis=-1, keepdims=True)
    h = (x * lax.rsqrt(ms + EPS)) * g2_ref[...]
    hb = (h * (1.0 + mod[4:5]) + mod[3:4]).astype(BF16)
    acc = jnp.zeros(x.shape, F32)
    for f in range(D_FF // D_MODEL):
        cs = slice(f * D_MODEL, (f + 1) * D_MODEL)
        u = jnp.maximum(jnp.dot(hb, wu_ref[:, cs], preferred_element_type=F32), 0.0)
        combine_chunk(f)
        acc = acc + jnp.dot((u * u).astype(BF16), wd_ref[cs, :], preferred_element_type=F32)
    out_ref[0] = x + mod[5:6] * acc


def _merge_mlp(x, mod, ya, o0, l0, o1, l1, o2, l2, gates, wa, wb, wo, g2, wu, wd):
    B, S, _ = x.shape
    tm = TM_OUT
    n = S // tm
    const = lambda b, i: (0, 0)
    tspec = lambda w: pl.BlockSpec((1, tm, w), lambda b, i: (b, jnp.maximum(i - 1, 0), 0))
    dspec = lambda w: pl.BlockSpec((1, tm, w), lambda b, i: (b, jnp.minimum(i, n - 1), 0))
    cspec = lambda a: pl.BlockSpec((1, 1) + a.shape[2:], lambda b, i: (b, jnp.minimum(i, n - 1), 0, 0))
    wspec = lambda a: pl.BlockSpec(a.shape, const, pipeline_mode=pl.Buffered(1))
    return pl.pallas_call(
        _merge_mlp_kernel,
        grid=(B, n + 1),
        in_specs=[tspec(D_MODEL), pl.BlockSpec((1, 6, D_MODEL), lambda b, i: (b, 0, 0)), tspec(DA_V),
                  dspec(DIL_W), dspec(DIL_W), cspec(o1), cspec(l1), cspec(o2), cspec(l2),
                  tspec(2 * D_MODEL), wspec(wa), wspec(wb), wspec(wo),
                  pl.BlockSpec((1, D_MODEL), const), wspec(wu), wspec(wd)],
        out_specs=tspec(D_MODEL),
        out_shape=jax.ShapeDtypeStruct((B, S, D_MODEL), F32),
        scratch_shapes=[pltpu.VMEM((2, DIL_W // LANES, tm, LANES), F32), pltpu.VMEM((tm, DIL_W), BF16)],
        compiler_params=_cparams(("arbitrary", "arbitrary"), 56),
        name="merge_mlp",
    )(x, mod, ya, o0, l0, o1, l1, o2, l2, gates, wa, wb, wo, g2, wu, wd)


def _pair_heads(a, b):
    d = a.shape[0]
    return jnp.stack([a.reshape(d, DA_HEADS, HEAD_DIM), b.reshape(d, DA_HEADS, HEAD_DIM)],
                     axis=2).reshape(d, 2 * DA_QK)


def _layer(x, mod, p):
    B, S, _ = x.shape
    qa, ka, va, dil0, dil1, dil2, gates = _inproj(x, mod, p["g1"], p["w_in"], p["gains"], p["gmat"])
    ya = _diff_attention(p["dac"], p["lam"], qa, ka, va, p["da_bias"], p["subln_g"])
    o0, l0 = _dilated_attention(p["dilc"], dil0.reshape(B, 1, S, 3 * DIL_W), p["dil_bias"], 0)
    o1, l1 = _dilated_attention(p["dilc"], dil1, p["dil_bias"], 1)
    o2, l2 = _dilated_attention(p["dilc"], dil2, p["dil_bias"], 2)
    rows = lambda a: a.reshape(B, a.shape[1], -1, LANES)
    return _merge_mlp(x, mod, ya, o0.reshape(B, S, DIL_W), l0.reshape(B, S, DIL_W),
                      rows(o1), rows(l1), rows(o2), rows(l2), gates,
                      p["w_br_a"], p["w_br_b"], p["w_o"], p["g2"], p["w_up"], p["w_down"])


def kernel(x_prompt, x_sample, c_prompt, c_sample, rel_bias, norm1_g, w_ada, b_ada, w_in, qn_a, kn_a,
           lambda_q1, lambda_k1, lambda_q2, lambda_k2, subln_g, qn_b, kn_b, w_br_a, w_br_b, w_o,
           norm2_g, w_up, w_down):
    nbp, nbs = c_prompt.shape[0], c_sample.shape[0]
    pad = (-(nbp + nbs)) % 8
    c_all = jnp.concatenate([c_prompt, c_sample, jnp.zeros((pad, D_MODEL), F32)], axis=0)
    mod = _modulation(c_all, w_ada[0], b_ada[0]).reshape(-1, 6, D_MODEL)

    w = w_in[0]
    q1, q2, k1, k2 = (w[:, n * DA_QK:(n + 1) * DA_QK] for n in range(4))
    w_perm = jnp.concatenate([_pair_heads(q1, q2), _pair_heads(k1, k2), w[:, 4 * DA_QK:]], axis=1)

    scale = HEAD_DIM ** -0.5
    tile8 = lambda v: jnp.tile(v, DIL_W // HEAD_DIM)
    gain_rows = [tile8(qn_a[0]) * (scale * LOG2E)] * 2 + [tile8(kn_a[0])] * 2
    for g in range(N_DIL):
        gain_rows += [tile8(qn_b[0, g]) * (scale * LOG2E), tile8(kn_b[0, g])]
    ids = jnp.arange(NORM_W) // HEAD_DIM
    gmat = jnp.where(ids[:, None] == ids[None, :], 1.0 / HEAD_DIM, 0.0).astype(BF16)

    def logit_reference(q_gain, k_gain, tab):
        qk_bound = HEAD_DIM * scale * jnp.max(jnp.abs(q_gain)) * jnp.max(jnp.abs(k_gain)) * BF16_ROUND_MARGIN
        c_ref = LOG2E * (qk_bound + jnp.max(tab, axis=0)) - DA_HEADROOM
        u_min = LOG2E * (jnp.min(tab, axis=0) - qk_bound) - c_ref
        return c_ref, jnp.all(u_min >= DA_MIN_EXP).astype(F32) * jnp.ones_like(c_ref)

    da_tab = rel_bias[:, :DA_HEADS]
    c_ref, da_ok = logit_reference(qn_a[0], kn_a[0], da_tab)
    dac = jnp.stack([c_ref, c_ref - LOG2E * da_tab[N_BUCKETS // 2 - 1], c_ref - LOG2E * da_tab[N_BUCKETS - 1],
                     da_ok], axis=1)
    dilc = jnp.stack([jnp.stack(logit_reference(
        qn_b[0, g], kn_b[0, g], rel_bias[:, DA_HEADS + g * DIL_HEADS:DA_HEADS + (g + 1) * DIL_HEADS]), axis=1)
        for g in range(N_DIL)], axis=0)

    p = {
        "g1": norm1_g[0].reshape(1, D_MODEL),
        "g2": norm2_g[0].reshape(1, D_MODEL),
        "w_in": w_perm.astype(BF16),
        "gains": jnp.stack(gain_rows, axis=0),
        "gmat": gmat,
        "lam": jnp.stack([lambda_q1[0], lambda_k1[0], lambda_q2[0], lambda_k2[0]], axis=0),
        "subln_g": subln_g[0].reshape(1, DA_VDIM),
        "dac": dac,
        "da_bias": _da_bias_tiles(rel_bias, dac),
        "dilc": dilc,
        "dil_bias": _dil_bias_tiles(rel_bias, dilc),
        "w_br_a": w_br_a[0].astype(BF16),
        "w_br_b": w_br_b[0].astype(BF16),
        "w_o": w_o[0].astype(BF16),
        "w_up": w_up[0].astype(BF16),
        "w_down": w_down[0].astype(BF16),
    }
    y_prompt = _layer(x_prompt, mod[:nbp], p)
    y_sample = _layer(x_sample, mod[nbp:nbp + nbs], p)
    return (y_prompt, y_sample)
```

```python
import functools
import math

import jax
import jax.numpy as jnp
from jax import lax
from jax.experimental import pallas as pl
from jax.experimental.pallas import tpu as pltpu

F32 = jnp.float32
BF16 = jnp.bfloat16

D_MODEL = 1024
HEAD_DIM = 64
DA_HEADS = 8
DA_VDIM = 2 * HEAD_DIM
DIL_CONFIG = ((128, 1), (512, 4), (2048, 16))
N_DIL = 3
DIL_HEADS = 8
D_FF = 4 * D_MODEL
N_BUCKETS = 32
NEG_INF = -1e30
EPS = 1e-6
LAM_INIT = 0.8 - 0.6 * math.exp(-0.3 * 0)

DA_QK = DA_HEADS * HEAD_DIM
DA_V = DA_HEADS * DA_VDIM
DIL_W = DIL_HEADS * HEAD_DIM
DIL_COLS = 3 * N_DIL * DIL_W
IN_COLS = 4 * DA_QK + DA_V + DIL_COLS + 2 * D_MODEL

LANES = 128
SUBLANES = 8
COL_BLK = 512
N_COL_BLKS = IN_COLS // COL_BLK
NORM_W = 256

TM_PROJ = 512
TM_OUT = 256
T_ATT = 512
TQ_ATT = 1024
TQ_DIL = 128
ROWS_DIL = 1024
HALF_DIL = 64
W_DIL = TQ_DIL + 2 * HALF_DIL

LOG2E = math.log2(math.e)
LN2 = math.log(2.0)
DA_HEADROOM = 64.0
DA_MIN_EXP = -60.0
BF16_ROUND_MARGIN = 1.02

_BUCKET_STEPS = (12, 16, 23, 32, 46, 64, 91)


def _cparams(sem, vmem_mb):
    return pltpu.CompilerParams(dimension_semantics=sem, vmem_limit_bytes=vmem_mb * 1024 * 1024)


def _mod_kernel(c_ref, w_ref, b_ref, o_ref):
    c = c_ref[...]
    a = c * (1.0 / (1.0 + jnp.exp(-c)))
    o_ref[...] = jnp.dot(a, w_ref[...], preferred_element_type=F32,
                         precision=lax.Precision.HIGHEST) + b_ref[...]


def _modulation(c_all, w_ada, b_ada):
    rows = c_all.shape[0]
    n = w_ada.shape[1]
    return pl.pallas_call(
        _mod_kernel,
        grid=(n // D_MODEL,),
        in_specs=[pl.BlockSpec((rows, D_MODEL), lambda j: (0, 0)),
                  pl.BlockSpec((D_MODEL, D_MODEL), lambda j: (0, j)),
                  pl.BlockSpec((1, D_MODEL), lambda j: (0, j))],
        out_specs=pl.BlockSpec((rows, D_MODEL), lambda j: (0, j)),
        out_shape=jax.ShapeDtypeStruct((rows, n), F32),
        compiler_params=_cparams(("arbitrary",), 32),
        name="modulation",
    )(c_all, w_ada, b_ada.reshape(1, n))


def _bias_lookup(rel, tab_ref, col):
    n = jnp.abs(rel)
    large = jnp.full(rel.shape, 8, jnp.int32)
    for t in _BUCKET_STEPS:
        large = large + (n >= t).astype(jnp.int32)
    bucket = jnp.where(n < 8, n, large) + jnp.where(rel > 0, N_BUCKETS // 2, 0)
    val = jnp.zeros(rel.shape, F32)
    for b in range(N_BUCKETS):
        val = jnp.where(bucket == b, tab_ref[b, col], val)
    return val


def _da_bias_kernel(tab_ref, dac_ref, o_ref):
    h = pl.program_id(0)
    sat = _BUCKET_STEPS[-1]
    row = lax.broadcasted_iota(jnp.int32, (LANES, LANES), 0)
    col = lax.broadcasted_iota(jnp.int32, (LANES, LANES), 1)
    far = (jnp.full((LANES, LANES), -dac_ref[h, 1], F32), jnp.full((LANES, LANES), -dac_ref[h, 2], F32))
    for d in range(-2, 3):
        for br in range(T_ATT // LANES):
            for bc in range(T_ATT // LANES):
                base = d * T_ATT + (bc - br) * LANES
                if base + LANES - 1 <= -sat:
                    val = far[0]
                elif base - (LANES - 1) >= sat:
                    val = far[1]
                else:
                    val = LOG2E * _bias_lookup(col - row + base, tab_ref, h) - dac_ref[h, 0]
                o_ref[0, d + 2, br * LANES:(br + 1) * LANES, bc * LANES:(bc + 1) * LANES] = val


def _da_bias_tiles(rel_bias, dac):
    assert T_ATT >= _BUCKET_STEPS[-1]
    return pl.pallas_call(
        _da_bias_kernel,
        grid=(DA_HEADS,),
        in_specs=[pl.BlockSpec(memory_space=pltpu.SMEM), pl.BlockSpec(memory_space=pltpu.SMEM)],
        out_specs=pl.BlockSpec((1, 5, T_ATT, T_ATT), lambda h: (h, 0, 0, 0)),
        out_shape=jax.ShapeDtypeStruct((DA_HEADS, 5, T_ATT, T_ATT), F32),
        compiler_params=_cparams(("arbitrary",), 32),
        name="da_bias_tiles",
    )(rel_bias, dac)


def _dil_bias_kernel(tab_ref, dilc_ref, o_ref):
    g = pl.program_id(0)
    v = pl.program_id(1)
    h = pl.program_id(2)
    row = lax.broadcasted_iota(jnp.int32, (TQ_DIL, W_DIL), 0)
    col = lax.broadcasted_iota(jnp.int32, (TQ_DIL, W_DIL), 1)
    rel = col - row - HALF_DIL * v
    dilation = lax.shift_left(jnp.int32(1), 2 * g)
    val = LOG2E * _bias_lookup(rel * dilation, tab_ref, DA_HEADS + g * DIL_HEADS + h) - dilc_ref[g, h, 0]
    o_ref[0, 0, 0] = jnp.where(jnp.abs(rel) <= HALF_DIL, val, NEG_INF)


def _dil_bias_tiles(rel_bias, dilc):
    return pl.pallas_call(
        _dil_bias_kernel,
        grid=(N_DIL, 3, DIL_HEADS),
        in_specs=[pl.BlockSpec(memory_space=pltpu.SMEM), pl.BlockSpec(memory_space=pltpu.SMEM)],
        out_specs=pl.BlockSpec((1, 1, 1, TQ_DIL, W_DIL), lambda g, v, h: (g, v, h, 0, 0)),
        out_shape=jax.ShapeDtypeStruct((N_DIL, 3, DIL_HEADS, TQ_DIL, W_DIL), F32),
        compiler_params=_cparams(("arbitrary",) * 3, 32),
        name="dil_bias_tiles",
    )(rel_bias, dilc)


_QK_BLOCKS = {0: 0, 1: 1, 2: 2, 3: 3}
for _g in range(N_DIL):
    _QK_BLOCKS[6 + 3 * _g] = 4 + 2 * _g
    _QK_BLOCKS[7 + 3 * _g] = 5 + 2 * _g


def _inproj_kernel(x_ref, mod_ref, g1_ref, w_ref, gain_ref, gmat_ref,
                   qa_ref, ka_ref, va_ref, dil0_ref, dil1_ref, dil2_ref, gate_ref, cls_ref):
    tm = x_ref.shape[1]
    dil_refs = (dil0_ref, dil1_ref, dil2_ref)
    x = x_ref[0]
    mod = mod_ref[0]
    ms = jnp.mean(x * x, axis=-1, keepdims=True)
    h = (x * lax.rsqrt(ms + EPS)) * g1_ref[...]
    h = h * (1.0 + mod[1:2]) + mod[0:1]
    gmat = gmat_ref[...]

    for k in range(D_MODEL // LANES):
        cls_ref[k] = h[:, k * LANES:(k + 1) * LANES]

    def class_major(r):
        chunks = [jnp.concatenate([cls_ref[k, pl.ds(c, tm // r, stride=r), :] for c in range(r)], axis=0)
                  for k in range(D_MODEL // LANES)]
        return jnp.concatenate(chunks, axis=1).astype(BF16)

    hb_by_dilation = {1: h.astype(BF16)}
    for _, r in DIL_CONFIG:
        if r not in hb_by_dilation:
            hb_by_dilation[r] = class_major(r)

    order = sorted(range(N_COL_BLKS), key=lambda b: (0 if b >= 15 else 1 if b in _QK_BLOCKS else 2, b))
    for blk in order:
        dilation = 1
        if blk < 6:
            dst, off = (qa_ref, ka_ref, va_ref)[blk // 2], (blk % 2) * COL_BLK
        elif blk < 15:
            g = (blk - 6) // 3
            dst, off, dilation = dil_refs[g], ((blk - 6) % 3) * COL_BLK, DIL_CONFIG[g][1]
        else:
            dst, off = gate_ref, (blk - 15) * COL_BLK
        acc = jnp.dot(hb_by_dilation[dilation], w_ref[:, blk * COL_BLK:(blk + 1) * COL_BLK],
                      preferred_element_type=F32)

        def put(lo, y):
            cols = slice(off + lo, off + lo + y.shape[1])
            if dilation == 1:
                dst[0, :, cols] = y.astype(BF16)
            else:
                n = tm // dilation
                for c in range(dilation):
                    dst[0, c, :, cols] = y[c * n:(c + 1) * n].astype(BF16)

        if blk in _QK_BLOCKS:
            gi = _QK_BLOCKS[blk]
            for half in range(COL_BLK // NORM_W):
                lo = half * NORM_W
                a = acc[:, lo:lo + NORM_W]
                hm = jnp.dot((a * a).astype(BF16), gmat, preferred_element_type=F32)
                put(lo, (a * lax.rsqrt(hm + EPS)) * gain_ref[gi:gi + 1, lo:lo + NORM_W])
        elif blk >= 15:
            put(0, 1.0 / (1.0 + jnp.exp(-acc)))
        else:
            put(0, acc)


def _inproj(x, mod, g1, w_in_b, gains, gmat):
    B, S, _ = x.shape
    tm = TM_PROJ
    const = lambda b, i: (0, 0)
    tok = lambda b, i: (b, i, 0)
    dil_shapes = tuple(jax.ShapeDtypeStruct((B, S, 3 * DIL_W) if r == 1 else (B, r, S // r, 3 * DIL_W), BF16)
                       for _, r in DIL_CONFIG)
    dil_specs = tuple(pl.BlockSpec((1, tm, 3 * DIL_W), tok) if r == 1 else
                      pl.BlockSpec((1, r, tm // r, 3 * DIL_W), lambda b, i: (b, 0, i, 0))
                      for _, r in DIL_CONFIG)
    tok_shapes = (jax.ShapeDtypeStruct((B, S, 2 * DA_QK), BF16),
                  jax.ShapeDtypeStruct((B, S, 2 * DA_QK), BF16),
                  jax.ShapeDtypeStruct((B, S, DA_V), BF16))
    gate_shape = jax.ShapeDtypeStruct((B, S, 2 * D_MODEL), BF16)
    tok_spec = lambda s: pl.BlockSpec((1, tm, s.shape[-1]), tok)
    return pl.pallas_call(
        _inproj_kernel,
        grid=(B, S // tm),
        in_specs=[pl.BlockSpec((1, tm, D_MODEL), tok),
                  pl.BlockSpec((1, 6, D_MODEL), lambda b, i: (b, 0, 0)),
                  pl.BlockSpec((1, D_MODEL), const),
                  pl.BlockSpec((D_MODEL, IN_COLS), const, pipeline_mode=pl.Buffered(1)),
                  pl.BlockSpec(gains.shape, const),
                  pl.BlockSpec((NORM_W, NORM_W), const)],
        out_specs=tuple(map(tok_spec, tok_shapes)) + dil_specs + (tok_spec(gate_shape),),
        out_shape=tok_shapes + dil_shapes + (gate_shape,),
        scratch_shapes=[pltpu.VMEM((D_MODEL // LANES, tm, LANES), F32)],
        compiler_params=_cparams(("arbitrary", "arbitrary"), 56),
        name="in_projection",
    )(x, mod, g1, w_in_b, gains, gmat)


def _diffattn_kernel(dac_ref, lam_ref, q_ref, k_ref, v_ref, bias_ref, g_ref, o_ref, vext_ref, acc_ref):
    T = T_ATT
    TQ = q_ref.shape[1]
    n_sub = TQ // T
    h = pl.program_id(1)
    i = pl.program_id(2)
    n_kv = k_ref.shape[1] // T

    @pl.when(i == 0)
    def _():
        vext_ref[:, :DA_VDIM] = v_ref[0]
        vext_ref[:, DA_VDIM:] = jnp.ones((vext_ref.shape[0], DA_VDIM), BF16)

    q = q_ref[0]
    lane = lax.broadcasted_iota(jnp.int32, q.shape, 1)
    zero = jnp.zeros_like(q)
    qs = jnp.concatenate([jnp.where(lane < HEAD_DIM, q, zero), jnp.where(lane >= HEAD_DIM, q, zero)], axis=0)

    def kv_rows(j):
        return slice(j * T, (j + 1) * T) if isinstance(j, int) else pl.ds(pl.multiple_of(j * T, T), T)

    def logits(j):
        s = lax.dot_general(qs, k_ref[0, kv_rows(j), :], (((1,), (1,)), ((), ())),
                            preferred_element_type=F32)
        tiles = [bias_ref[0, jnp.clip(j - (i * n_sub + t), -2, 2) + 2] for t in range(n_sub)]
        return s + jnp.concatenate(tiles + tiles, axis=0)

    fast_ok = dac_ref[h, 3] > 0.5

    @pl.when(fast_ok)
    def _():
        for j in range(n_kv):
            pv = jnp.dot(jnp.exp2(logits(j)).astype(BF16), vext_ref[j * T:(j + 1) * T, :],
                         preferred_element_type=F32)
            if j == 0:
                acc_ref[...] = pv
            else:
                acc_ref[...] += pv

    @pl.when(jnp.logical_not(fast_ok))
    def _():
        def body(j, carry):
            m, acc = carry
            u = logits(j)
            m_new = jnp.maximum(m, jnp.max(u, axis=-1, keepdims=True))
            p = jnp.exp2(u - m_new)
            acc = jnp.exp2(m - m_new) * acc + jnp.dot(p.astype(BF16), vext_ref[kv_rows(j), :],
                                                       preferred_element_type=F32)
            return m_new, acc

        init = (jnp.full((2 * TQ, 1), NEG_INF, F32), jnp.zeros(acc_ref.shape, F32))
        acc_ref[...] = lax.fori_loop(0, n_kv, body, init)[1]

    lp = lam_ref[...]
    lam = (jnp.exp(jnp.sum(lp[0:1] * lp[1:2], axis=-1, keepdims=True))
           - jnp.exp(jnp.sum(lp[2:3] * lp[3:4], axis=-1, keepdims=True)) + LAM_INIT)
    o = (acc_ref[:TQ, :DA_VDIM] / acc_ref[:TQ, DA_VDIM:]
         - lam * (acc_ref[TQ:, :DA_VDIM] / acc_ref[TQ:, DA_VDIM:]))
    ms = jnp.mean(o * o, axis=-1, keepdims=True)
    o = (o * lax.rsqrt(ms + EPS)) * g_ref[...] * (1.0 - LAM_INIT)
    o_ref[0] = o.astype(o_ref.dtype)


def _diff_attention(dac, lam_params, qa, ka, va, bias_tiles, subln_g):
    B, S, _ = qa.shape
    T = T_ATT
    TQ = TQ_ATT
    return pl.pallas_call(
        _diffattn_kernel,
        grid=(B, DA_HEADS, S // TQ),
        in_specs=[pl.BlockSpec(memory_space=pltpu.SMEM),
                  pl.BlockSpec((4, HEAD_DIM), lambda b, h, i: (0, 0)),
                  pl.BlockSpec((1, TQ, LANES), lambda b, h, i: (b, i, h)),
                  pl.BlockSpec((1, S, LANES), lambda b, h, i: (b, 0, h)),
                  pl.BlockSpec((1, S, LANES), lambda b, h, i: (b, 0, h)),
                  pl.BlockSpec((1, 5, T, T), lambda b, h, i: (h, 0, 0, 0)),
                  pl.BlockSpec((1, DA_VDIM), lambda b, h, i: (0, 0))],
        out_specs=pl.BlockSpec((1, TQ, LANES), lambda b, h, i: (b, i, h)),
        out_shape=jax.ShapeDtypeStruct((B, S, DA_V), BF16),
        scratch_shapes=[pltpu.VMEM((S, 2 * DA_VDIM), BF16), pltpu.VMEM((2 * TQ, 2 * DA_VDIM), F32)],
        compiler_params=_cparams(("arbitrary",) * 3, 56),
        name="diff_attention",
    )(dac, lam_params, qa, ka, va, bias_tiles, subln_g)


def _dilattn_kernel(g, dilc_ref, q_ref, k_ref, v_ref, bias_ref, o_ref, lse_ref):
    TQ = TQ_DIL
    L = k_ref.shape[2]
    tiles_per_step = q_ref.shape[2] // TQ
    n_q = L // TQ
    low = lax.broadcasted_iota(jnp.int32, (TQ, LANES), 1) < HEAD_DIM
    ones = jnp.ones((W_DIL, LANES), BF16)

    def run(use_max):
        for sub in range(tiles_per_step):
            for hp in range(DIL_HEADS // 2):
                one_tile(use_max, sub, hp)

    def one_tile(use_max, sub, hp):
            i = pl.program_id(2) * tiles_per_step + sub
            w0 = pl.multiple_of(jnp.clip(i * TQ - HALF_DIL, 0, L - W_DIL), HALF_DIL)
            variant = jnp.where(i == 0, 0, jnp.where(i == n_q - 1, 2, 1))
            cs = slice(hp * LANES, (hp + 1) * LANES)
            qp = q_ref[0, 0, sub * TQ:(sub + 1) * TQ, cs]
            kp = k_ref[0, 0, pl.ds(w0, W_DIL), cs]
            vext = jnp.concatenate([v_ref[0, 0, pl.ds(w0, W_DIL), cs], ones], axis=1)
            zero = jnp.zeros_like(qp)
            qs = jnp.concatenate([jnp.where(low, qp, zero), jnp.where(low, zero, qp)], axis=0)
            u = lax.dot_general(qs, kp, (((1,), (1,)), ((), ())), preferred_element_type=F32)
            u = u + jnp.concatenate([bias_ref[0, variant, 2 * hp], bias_ref[0, variant, 2 * hp + 1]], axis=0)
            if use_max:
                m = jnp.max(u, axis=-1, keepdims=True)
                u = u - m
            r = jnp.dot(jnp.exp2(u).astype(BF16), vext, preferred_element_type=F32)
            num, den = r[:, :LANES], r[:, LANES:]
            lse2 = jnp.log2(den)
            if use_max:
                lse2 = lse2 + m
            o_pair = jnp.where(low, num[:TQ] / den[:TQ], num[TQ:] / den[TQ:]).astype(o_ref.dtype)
            lse_pair = LN2 * jnp.where(low, lse2[:TQ] + dilc_ref[g, 2 * hp, 0], lse2[TQ:] + dilc_ref[g, 2 * hp + 1, 0])
            if len(o_ref.shape) == 7:
                n_grp = o_ref.shape[3]
                for t in range(TQ // (n_grp * SUBLANES)):
                    for grp in range(n_grp):
                        lo = (t * n_grp + grp) * SUBLANES
                        o_ref[0, sub * (TQ // (n_grp * SUBLANES)) + t, hp, grp, 0] = o_pair[lo:lo + SUBLANES]
                        lse_ref[0, sub * (TQ // (n_grp * SUBLANES)) + t, hp, grp, 0] = lse_pair[lo:lo + SUBLANES]
            else:
                o_ref[0, 0, sub * TQ:(sub + 1) * TQ, cs] = o_pair
                lse_ref[0, 0, sub * TQ:(sub + 1) * TQ, cs] = lse_pair

    fast_ok = dilc_ref[g, 0, 1] > 0.5
    pl.when(fast_ok)(functools.partial(run, False))
    pl.when(jnp.logical_not(fast_ok))(functools.partial(run, True))


def _dilated_attention(dilc, dil, bias_tiles, g):
    B, r, L, _ = dil.shape
    kv_mode = pl.Buffered(1) if L * DIL_W * 2 > (4 << 20) else pl.Buffered(2)
    qmap = lambda b, c, i: (b, c, i, 0)
    rows = min(ROWS_DIL, L)
    if r == 1:
        o_spec = pl.BlockSpec((1, 1, rows, DIL_W), qmap)
        o_shape, o_dtype = (B, r, L, DIL_W), BF16
    else:
        n = TM_OUT // r
        nk = DIL_W // LANES
        o_spec = pl.BlockSpec((1, rows // n, nk, n // SUBLANES, 1, SUBLANES, LANES),
                              lambda b, c, i: (b, i, 0, 0, c, 0, 0))
        o_shape, o_dtype = (B, L // n, nk, n // SUBLANES, r, SUBLANES, LANES), F32
    return pl.pallas_call(
        functools.partial(_dilattn_kernel, g),
        grid=(B, r, L // rows),
        in_specs=[pl.BlockSpec(memory_space=pltpu.SMEM),
                  pl.BlockSpec((1, 1, rows, DIL_W), qmap),
                  pl.BlockSpec((1, 1, L, DIL_W), lambda b, c, i: (b, c, 0, 1), pipeline_mode=kv_mode),
                  pl.BlockSpec((1, 1, L, DIL_W), lambda b, c, i: (b, c, 0, 2), pipeline_mode=kv_mode),
                  pl.BlockSpec((1, 3, DIL_HEADS, TQ_DIL, W_DIL), lambda b, c, i: (g, 0, 0, 0, 0))],
        out_specs=(o_spec, o_spec),
        out_shape=(jax.ShapeDtypeStruct(o_shape, o_dtype), jax.ShapeDtypeStruct(o_shape, F32)),
        compiler_params=_cparams(("arbitrary",) * 3, 48),
        name=f"dilated_attention_r{r}",
    )(dilc, dil, dil, dil, bias_tiles)


def _interleave_classes(ref, r, k, il_ref):
    nk = DIL_W // LANES
    n = ref.shape[2] // (r * nk)
    pieces = [ref[0, 0, pl.ds(k * n * r + (l // SUBLANES) * r * SUBLANES + l % SUBLANES, r, stride=SUBLANES), :]
              for l in range(n)]
    if r % 8 == 0:
        return jnp.concatenate(pieces, axis=0)
    for l in range(n):
        il_ref[pl.ds(l * r, r), :] = pieces[l]
    return il_ref[...]


def _merge_mlp_kernel(x_ref, mod_ref, ya_ref, o0_ref, l0_ref, o1_ref, l1_ref, o2_ref, l2_ref, gate_ref,
                      wa_ref, wb_ref, wo_ref, g2_ref, wu_ref, wd_ref, out_ref, il_ref, yb_ref):
    mod = mod_ref[0]

    @pl.when(pl.program_id(1) == 0)
    def _():
        yb_ref[...] = jnp.zeros(yb_ref.shape, BF16)

    yb = yb_ref[...]
    r1, r2 = DIL_CONFIG[1][1], DIL_CONFIG[2][1]

    def combine_chunk(k):
        cs = slice(k * LANES, (k + 1) * LANES)
        outs = (o0_ref[0, :, cs].astype(F32), _interleave_classes(o1_ref, r1, k, il_ref.at[0, k]),
                _interleave_classes(o2_ref, r2, k, il_ref.at[0, k]))
        lses = (l0_ref[0, :, cs], _interleave_classes(l1_ref, r1, k, il_ref.at[1, k]),
                _interleave_classes(l2_ref, r2, k, il_ref.at[1, k]))
        mx = jnp.maximum(jnp.maximum(lses[0], lses[1]), lses[2])
        es = [jnp.exp(l - mx) for l in lses]
        den = es[0] + es[1] + es[2]
        yb_ref[:, cs] = ((es[0] * outs[0] + es[1] * outs[1] + es[2] * outs[2]) / den).astype(BF16)

    pa = jnp.dot(ya_ref[0], wa_ref[...], preferred_element_type=F32)
    pb = jnp.dot(yb, wb_ref[...], preferred_element_type=F32)
    gates = gate_ref[0]
    merged = gates[:, :D_MODEL].astype(F32) * pa + gates[:, D_MODEL:].astype(F32) * pb
    x = x_ref[0] + mod[2:3] * jnp.dot(merged.astype(BF16), wo_ref[...], preferred_element_type=F32)

    ms = jnp.mean(x * x, axis=-1, keepdims=True)
    h = (x * lax.rsqrt(ms + EPS)) * g2_ref[...]
    hb = (h * (1.0 + mod[4:5]) + mod[3:4]).astype(BF16)
    acc = jnp.zeros(x.shape, F32)
    for f in range(D_FF // D_MODEL):
        cs = slice(f * D_MODEL, (f + 1) * D_MODEL)
        u = jnp.maximum(jnp.dot(hb, wu_ref[:, cs], preferred_element_type=F32), 0.0)
        combine_chunk(f)
        acc = acc + jnp.dot((u * u).astype(BF16), wd_ref[cs, :], preferred_element_type=F32)
    out_ref[0] = x + mod[5:6] * acc


def _merge_mlp(x, mod, ya, o0, l0, o1, l1, o2, l2, gates, wa, wb, wo, g2, wu, wd):
    B, S, _ = x.shape
    tm = TM_OUT
    n = S // tm
    const = lambda b, i: (0, 0)
    tspec = lambda w: pl.BlockSpec((1, tm, w), lambda b, i: (b, jnp.maximum(i - 1, 0), 0))
    dspec = lambda w: pl.BlockSpec((1, tm, w), lambda b, i: (b, jnp.minimum(i, n - 1), 0))
    cspec = lambda a: pl.BlockSpec((1, 1) + a.shape[2:], lambda b, i: (b, jnp.minimum(i, n - 1), 0, 0))
    wspec = lambda a: pl.BlockSpec(a.shape, const, pipeline_mode=pl.Buffered(1))
    return pl.pallas_call(
        _merge_mlp_kernel,
        grid=(B, n + 1),
        in_specs=[tspec(D_MODEL), pl.BlockSpec((1, 6, D_MODEL), lambda b, i: (b, 0, 0)), tspec(DA_V),
                  dspec(DIL_W), dspec(DIL_W), cspec(o1), cspec(l1), cspec(o2), cspec(l2),
                  tspec(2 * D_MODEL), wspec(wa), wspec(wb), wspec(wo),
                  pl.BlockSpec((1, D_MODEL), const), wspec(wu), wspec(wd)],
        out_specs=tspec(D_MODEL),
        out_shape=jax.ShapeDtypeStruct((B, S, D_MODEL), F32),
        scratch_shapes=[pltpu.VMEM((2, DIL_W // LANES, tm, LANES), F32), pltpu.VMEM((tm, DIL_W), BF16)],
        compiler_params=_cparams(("arbitrary", "arbitrary"), 56),
        name="merge_mlp",
    )(x, mod, ya, o0, l0, o1, l1, o2, l2, gates, wa, wb, wo, g2, wu, wd)


def _pair_heads(a, b):
    d = a.shape[0]
    return jnp.stack([a.reshape(d, DA_HEADS, HEAD_DIM), b.reshape(d, DA_HEADS, HEAD_DIM)],
                     axis=2).reshape(d, 2 * DA_QK)


def _layer(x, mod, p):
    B, S, _ = x.shape
    qa, ka, va, dil0, dil1, dil2, gates = _inproj(x, mod, p["g1"], p["w_in"], p["gains"], p["gmat"])
    ya = _diff_attention(p["dac"], p["lam"], qa, ka, va, p["da_bias"], p["subln_g"])
    o0, l0 = _dilated_attention(p["dilc"], dil0.reshape(B, 1, S, 3 * DIL_W), p["dil_bias"], 0)
    o1, l1 = _dilated_attention(p["dilc"], dil1, p["dil_bias"], 1)
    o2, l2 = _dilated_attention(p["dilc"], dil2, p["dil_bias"], 2)
    rows = lambda a: a.reshape(B, a.shape[1], -1, LANES)
    return _merge_mlp(x, mod, ya, o0.reshape(B, S, DIL_W), l0.reshape(B, S, DIL_W),
                      rows(o1), rows(l1), rows(o2), rows(l2), gates,
                      p["w_br_a"], p["w_br_b"], p["w_o"], p["g2"], p["w_up"], p["w_down"])


def kernel(x_prompt, x_sample, c_prompt, c_sample, rel_bias, norm1_g, w_ada, b_ada, w_in, qn_a, kn_a,
           lambda_q1, lambda_k1, lambda_q2, lambda_k2, subln_g, qn_b, kn_b, w_br_a, w_br_b, w_o,
           norm2_g, w_up, w_down):
    nbp, nbs = c_prompt.shape[0], c_sample.shape[0]
    pad = (-(nbp + nbs)) % 8
    c_all = jnp.concatenate([c_prompt, c_sample, jnp.zeros((pad, D_MODEL), F32)], axis=0)
    mod = _modulation(c_all, w_ada[0], b_ada[0]).reshape(-1, 6, D_MODEL)

    w = w_in[0]
    q1, q2, k1, k2 = (w[:, n * DA_QK:(n + 1) * DA_QK] for n in range(4))
    w_perm = jnp.concatenate([_pair_heads(q1, q2), _pair_heads(k1, k2), w[:, 4 * DA_QK:]], axis=1)

    scale = HEAD_DIM ** -0.5
    tile8 = lambda v: jnp.tile(v, DIL_W // HEAD_DIM)
    gain_rows = [tile8(qn_a[0]) * (scale * LOG2E)] * 2 + [tile8(kn_a[0])] * 2
    for g in range(N_DIL):
        gain_rows += [tile8(qn_b[0, g]) * (scale * LOG2E), tile8(kn_b[0, g])]
    ids = jnp.arange(NORM_W) // HEAD_DIM
    gmat = jnp.where(ids[:, None] == ids[None, :], 1.0 / HEAD_DIM, 0.0).astype(BF16)

    def logit_reference(q_gain, k_gain, tab):
        qk_bound = HEAD_DIM * scale * jnp.max(jnp.abs(q_gain)) * jnp.max(jnp.abs(k_gain)) * BF16_ROUND_MARGIN
        c_ref = LOG2E * (qk_bound + jnp.max(tab, axis=0)) - DA_HEADROOM
        u_min = LOG2E * (jnp.min(tab, axis=0) - qk_bound) - c_ref
        return c_ref, jnp.all(u_min >= DA_MIN_EXP).astype(F32) * jnp.ones_like(c_ref)

    da_tab = rel_bias[:, :DA_HEADS]
    c_ref, da_ok = logit_reference(qn_a[0], kn_a[0], da_tab)
    dac = jnp.stack([c_ref, c_ref - LOG2E * da_tab[N_BUCKETS // 2 - 1], c_ref - LOG2E * da_tab[N_BUCKETS - 1],
                     da_ok], axis=1)
    dilc = jnp.stack([jnp.stack(logit_reference(
        qn_b[0, g], kn_b[0, g], rel_bias[:, DA_HEADS + g * DIL_HEADS:DA_HEADS + (g + 1) * DIL_HEADS]), axis=1)
        for g in range(N_DIL)], axis=0)

    p = {
        "g1": norm1_g[0].reshape(1, D_MODEL),
        "g2": norm2_g[0].reshape(1, D_MODEL),
        "w_in": w_perm.astype(BF16),
        "gains": jnp.stack(gain_rows, axis=0),
        "gmat": gmat,
        "lam": jnp.stack([lambda_q1[0], lambda_k1[0], lambda_q2[0], lambda_k2[0]], axis=0),
        "subln_g": subln_g[0].reshape(1, DA_VDIM),
        "dac": dac,
        "da_bias": _da_bias_tiles(rel_bias, dac),
        "dilc": dilc,
        "dil_bias": _dil_bias_tiles(rel_bias, dilc),
        "w_br_a": w_br_a[0].astype(BF16),
        "w_br_b": w_br_b[0].astype(BF16),
        "w_o": w_o[0].astype(BF16),
        "w_up": w_up[0].astype(BF16),
        "w_down": w_down[0].astype(BF16),
    }
    y_prompt = _layer(x_prompt, mod[:nbp], p)
    y_sample = _layer(x_sample, mod[nbp:nbp + nbs], p)
    return (y_prompt, y_sample)
```

```python
import functools
import math

import jax
import jax.numpy as jnp
from jax import lax
from jax.experimental import pallas as pl
from jax.experimental.pallas import tpu as pltpu

F32 = jnp.float32
BF16 = jnp.bfloat16

D_MODEL = 1024
HEAD_DIM = 64
DA_HEADS = 8
DA_VDIM = 2 * HEAD_DIM
DIL_CONFIG = ((128, 1), (512, 4), (2048, 16))
N_DIL = 3
DIL_HEADS = 8
D_FF = 4 * D_MODEL
N_BUCKETS = 32
NEG_INF = -1e30
EPS = 1e-6
LAM_INIT = 0.8 - 0.6 * math.exp(-0.3 * 0)

DA_QK = DA_HEADS * HEAD_DIM
DA_V = DA_HEADS * DA_VDIM
DIL_W = DIL_HEADS * HEAD_DIM
DIL_COLS = 3 * N_DIL * DIL_W
IN_COLS = 4 * DA_QK + DA_V + DIL_COLS + 2 * D_MODEL

LANES = 128
SUBLANES = 8
COL_BLK = 512
N_COL_BLKS = IN_COLS // COL_BLK
NORM_W = 256

TM_PROJ = 512
TM_OUT = 256
T_ATT = 512
TQ_ATT = 1024
TQ_DIL = 128
ROWS_DIL = 1024
HALF_DIL = 64
W_DIL = TQ_DIL + 2 * HALF_DIL

LOG2E = math.log2(math.e)
LN2 = math.log(2.0)
DA_HEADROOM = 64.0
DA_MIN_EXP = -60.0
BF16_ROUND_MARGIN = 1.02

_MAX_EXACT = N_BUCKETS // 4
_BUCKET_STEPS = (12, 16, 23, 32, 46, 64, 91)


def _cparams(sem, vmem_mb):
    return pltpu.CompilerParams(dimension_semantics=sem, vmem_limit_bytes=vmem_mb * 1024 * 1024)


def _mod_kernel(c_ref, w_ref, b_ref, o_ref):
    c = c_ref[...]
    a = c * (1.0 / (1.0 + jnp.exp(-c)))
    o_ref[...] = jnp.dot(a, w_ref[...], preferred_element_type=F32,
                         precision=lax.Precision.HIGHEST) + b_ref[...]


def _modulation(c_all, w_ada, b_ada):
    rows = c_all.shape[0]
    n = w_ada.shape[1]
    return pl.pallas_call(
        _mod_kernel,
        grid=(n // D_MODEL,),
        in_specs=[pl.BlockSpec((rows, D_MODEL), lambda j: (0, 0)),
                  pl.BlockSpec((D_MODEL, D_MODEL), lambda j: (0, j)),
                  pl.BlockSpec((1, D_MODEL), lambda j: (0, j))],
        out_specs=pl.BlockSpec((rows, D_MODEL), lambda j: (0, j)),
        out_shape=jax.ShapeDtypeStruct((rows, n), F32),
        compiler_params=_cparams(("arbitrary",), 32),
        name="modulation",
    )(c_all, w_ada, b_ada.reshape(1, n))


def _bias_lookup(rel, tab_ref, col):
    n = jnp.abs(rel)
    large = jnp.full(rel.shape, _MAX_EXACT, jnp.int32)
    for t in _BUCKET_STEPS:
        large = large + (n >= t).astype(jnp.int32)
    bucket = jnp.where(n < _MAX_EXACT, n, large) + jnp.where(rel > 0, N_BUCKETS // 2, 0)
    val = jnp.zeros(rel.shape, F32)
    for b in range(N_BUCKETS):
        val = jnp.where(bucket == b, tab_ref[b, col], val)
    return val


def _da_bias_kernel(tab_ref, dac_ref, o_ref):
    h = pl.program_id(0)
    sat = _BUCKET_STEPS[-1]
    row = lax.broadcasted_iota(jnp.int32, (LANES, LANES), 0)
    col = lax.broadcasted_iota(jnp.int32, (LANES, LANES), 1)
    far = (jnp.full((LANES, LANES), -dac_ref[h, 1], F32), jnp.full((LANES, LANES), -dac_ref[h, 2], F32))
    for d in range(-2, 3):
        for br in range(T_ATT // LANES):
            for bc in range(T_ATT // LANES):
                base = d * T_ATT + (bc - br) * LANES
                if base + LANES - 1 <= -sat:
                    val = far[0]
                elif base - (LANES - 1) >= sat:
                    val = far[1]
                else:
                    val = LOG2E * _bias_lookup(col - row + base, tab_ref, h) - dac_ref[h, 0]
                o_ref[0, d + 2, br * LANES:(br + 1) * LANES, bc * LANES:(bc + 1) * LANES] = val


def _da_bias_tiles(rel_bias, dac):
    assert T_ATT >= _BUCKET_STEPS[-1]
    return pl.pallas_call(
        _da_bias_kernel,
        grid=(DA_HEADS,),
        in_specs=[pl.BlockSpec(memory_space=pltpu.SMEM), pl.BlockSpec(memory_space=pltpu.SMEM)],
        out_specs=pl.BlockSpec((1, 5, T_ATT, T_ATT), lambda h: (h, 0, 0, 0)),
        out_shape=jax.ShapeDtypeStruct((DA_HEADS, 5, T_ATT, T_ATT), F32),
        compiler_params=_cparams(("arbitrary",), 32),
        name="da_bias_tiles",
    )(rel_bias, dac)


def _dil_bias_kernel(tab_ref, dilc_ref, o_ref):
    g = pl.program_id(0)
    v = pl.program_id(1)
    h = pl.program_id(2)
    row = lax.broadcasted_iota(jnp.int32, (TQ_DIL, W_DIL), 0)
    col = lax.broadcasted_iota(jnp.int32, (TQ_DIL, W_DIL), 1)
    rel = col - row - HALF_DIL * v
    dilation = lax.shift_left(jnp.int32(1), 2 * g)
    val = LOG2E * _bias_lookup(rel * dilation, tab_ref, DA_HEADS + g * DIL_HEADS + h) - dilc_ref[g, h, 0]
    o_ref[0, 0, 0] = jnp.where(jnp.abs(rel) <= HALF_DIL, val, NEG_INF)


def _dil_bias_tiles(rel_bias, dilc):
    return pl.pallas_call(
        _dil_bias_kernel,
        grid=(N_DIL, 3, DIL_HEADS),
        in_specs=[pl.BlockSpec(memory_space=pltpu.SMEM), pl.BlockSpec(memory_space=pltpu.SMEM)],
        out_specs=pl.BlockSpec((1, 1, 1, TQ_DIL, W_DIL), lambda g, v, h: (g, v, h, 0, 0)),
        out_shape=jax.ShapeDtypeStruct((N_DIL, 3, DIL_HEADS, TQ_DIL, W_DIL), F32),
        compiler_params=_cparams(("arbitrary",) * 3, 32),
        name="dil_bias_tiles",
    )(rel_bias, dilc)


_DIL_BLK0 = (4 * DA_QK + DA_V) // COL_BLK
_GATE_BLK0 = _DIL_BLK0 + DIL_COLS // COL_BLK
_QK_BLOCKS = {0: 0, 1: 1, 2: 2, 3: 3}
for _g in range(N_DIL):
    _QK_BLOCKS[_DIL_BLK0 + 3 * _g] = 4 + 2 * _g
    _QK_BLOCKS[_DIL_BLK0 + 3 * _g + 1] = 5 + 2 * _g


def _inproj_kernel(x_ref, mod_ref, g1_ref, w_ref, gain_ref, gmat_ref,
                   qa_ref, ka_ref, va_ref, dil0_ref, dil1_ref, dil2_ref, gate_ref, cls_ref):
    tm = x_ref.shape[1]
    dil_refs = (dil0_ref, dil1_ref, dil2_ref)
    x = x_ref[0]
    mod = mod_ref[0]
    ms = jnp.mean(x * x, axis=-1, keepdims=True)
    h = (x * lax.rsqrt(ms + EPS)) * g1_ref[...]
    h = h * (1.0 + mod[1:2]) + mod[0:1]
    gmat = gmat_ref[...]

    for k in range(D_MODEL // LANES):
        cls_ref[k] = h[:, k * LANES:(k + 1) * LANES]

    def class_major(r):
        chunks = [jnp.concatenate([cls_ref[k, pl.ds(c, tm // r, stride=r), :] for c in range(r)], axis=0)
                  for k in range(D_MODEL // LANES)]
        return jnp.concatenate(chunks, axis=1).astype(BF16)

    hb_by_dilation = {1: h.astype(BF16)}
    for _, r in DIL_CONFIG:
        if r not in hb_by_dilation:
            hb_by_dilation[r] = class_major(r)

    order = sorted(range(N_COL_BLKS), key=lambda b: (0 if b >= _GATE_BLK0 else 1 if b in _QK_BLOCKS else 2, b))
    for blk in order:
        dilation = 1
        if blk < _DIL_BLK0:
            dst, off = (qa_ref, ka_ref, va_ref)[blk // 2], (blk % 2) * COL_BLK
        elif blk < _GATE_BLK0:
            g = (blk - _DIL_BLK0) // 3
            dst, off, dilation = dil_refs[g], ((blk - _DIL_BLK0) % 3) * COL_BLK, DIL_CONFIG[g][1]
        else:
            dst, off = gate_ref, (blk - _GATE_BLK0) * COL_BLK
        acc = jnp.dot(hb_by_dilation[dilation], w_ref[:, blk * COL_BLK:(blk + 1) * COL_BLK],
                      preferred_element_type=F32)

        def put(lo, y):
            cols = slice(off + lo, off + lo + y.shape[1])
            if dilation == 1:
                dst[0, :, cols] = y.astype(BF16)
            else:
                n = tm // dilation
                for c in range(dilation):
                    dst[0, c, :, cols] = y[c * n:(c + 1) * n].astype(BF16)

        if blk in _QK_BLOCKS:
            gi = _QK_BLOCKS[blk]
            for half in range(COL_BLK // NORM_W):
                lo = half * NORM_W
                a = acc[:, lo:lo + NORM_W]
                hm = jnp.dot((a * a).astype(BF16), gmat, preferred_element_type=F32)
                put(lo, (a * lax.rsqrt(hm + EPS)) * gain_ref[gi:gi + 1, lo:lo + NORM_W])
        elif blk >= _GATE_BLK0:
            put(0, 1.0 / (1.0 + jnp.exp(-acc)))
        else:
            put(0, acc)


def _inproj(x, mod, g1, w_in_b, gains, gmat):
    B, S, _ = x.shape
    tm = TM_PROJ
    const = lambda b, i: (0, 0)
    tok = lambda b, i: (b, i, 0)
    dil_shapes = tuple(jax.ShapeDtypeStruct((B, S, 3 * DIL_W) if r == 1 else (B, r, S // r, 3 * DIL_W), BF16)
                       for _, r in DIL_CONFIG)
    dil_specs = tuple(pl.BlockSpec((1, tm, 3 * DIL_W), tok) if r == 1 else
                      pl.BlockSpec((1, r, tm // r, 3 * DIL_W), lambda b, i: (b, 0, i, 0))
                      for _, r in DIL_CONFIG)
    tok_shapes = (jax.ShapeDtypeStruct((B, S, 2 * DA_QK), BF16),
                  jax.ShapeDtypeStruct((B, S, 2 * DA_QK), BF16),
                  jax.ShapeDtypeStruct((B, S, DA_V), BF16))
    gate_shape = jax.ShapeDtypeStruct((B, S, 2 * D_MODEL), BF16)
    tok_spec = lambda s: pl.BlockSpec((1, tm, s.shape[-1]), tok)
    return pl.pallas_call(
        _inproj_kernel,
        grid=(B, S // tm),
        in_specs=[pl.BlockSpec((1, tm, D_MODEL), tok),
                  pl.BlockSpec((1, 6, D_MODEL), lambda b, i: (b, 0, 0)),
                  pl.BlockSpec((1, D_MODEL), const),
                  pl.BlockSpec((D_MODEL, IN_COLS), const, pipeline_mode=pl.Buffered(1)),
                  pl.BlockSpec(gains.shape, const),
                  pl.BlockSpec((NORM_W, NORM_W), const)],
        out_specs=tuple(map(tok_spec, tok_shapes)) + dil_specs + (tok_spec(gate_shape),),
        out_shape=tok_shapes + dil_shapes + (gate_shape,),
        scratch_shapes=[pltpu.VMEM((D_MODEL // LANES, tm, LANES), F32)],
        compiler_params=_cparams(("arbitrary", "arbitrary"), 56),
        name="in_projection",
    )(x, mod, g1, w_in_b, gains, gmat)


def _diffattn_kernel(dac_ref, lam_ref, q_ref, k_ref, v_ref, bias_ref, g_ref, o_ref, vext_ref, acc_ref):
    T = T_ATT
    TQ = q_ref.shape[1]
    n_sub = TQ // T
    h = pl.program_id(1)
    i = pl.program_id(2)
    n_kv = k_ref.shape[1] // T

    @pl.when(i == 0)
    def _():
        vext_ref[:, :DA_VDIM] = v_ref[0]
        vext_ref[:, DA_VDIM:] = jnp.ones((vext_ref.shape[0], DA_VDIM), BF16)

    q = q_ref[0]
    lane = lax.broadcasted_iota(jnp.int32, q.shape, 1)
    zero = jnp.zeros_like(q)
    qs = jnp.concatenate([jnp.where(lane < HEAD_DIM, q, zero), jnp.where(lane >= HEAD_DIM, q, zero)], axis=0)

    def kv_rows(j):
        return slice(j * T, (j + 1) * T) if isinstance(j, int) else pl.ds(pl.multiple_of(j * T, T), T)

    def logits(j):
        s = lax.dot_general(qs, k_ref[0, kv_rows(j), :], (((1,), (1,)), ((), ())),
                            preferred_element_type=F32)
        tiles = [bias_ref[0, jnp.clip(j - (i * n_sub + t), -2, 2) + 2] for t in range(n_sub)]
        return s + jnp.concatenate(tiles + tiles, axis=0)

    fast_ok = dac_ref[h, 3] > 0.5

    @pl.when(fast_ok)
    def _():
        for j in range(n_kv):
            pv = jnp.dot(jnp.exp2(logits(j)).astype(BF16), vext_ref[j * T:(j + 1) * T, :],
                         preferred_element_type=F32)
            if j == 0:
                acc_ref[...] = pv
            else:
                acc_ref[...] += pv

    @pl.when(jnp.logical_not(fast_ok))
    def _():
        def body(j, carry):
            m, acc = carry
            u = logits(j)
            m_new = jnp.maximum(m, jnp.max(u, axis=-1, keepdims=True))
            p = jnp.exp2(u - m_new)
            acc = jnp.exp2(m - m_new) * acc + jnp.dot(p.astype(BF16), vext_ref[kv_rows(j), :],
                                                       preferred_element_type=F32)
            return m_new, acc

        init = (jnp.full((2 * TQ, 1), NEG_INF, F32), jnp.zeros(acc_ref.shape, F32))
        acc_ref[...] = lax.fori_loop(0, n_kv, body, init)[1]

    lp = lam_ref[...]
    lam = (jnp.exp(jnp.sum(lp[0:1] * lp[1:2], axis=-1, keepdims=True))
           - jnp.exp(jnp.sum(lp[2:3] * lp[3:4], axis=-1, keepdims=True)) + LAM_INIT)
    o = (acc_ref[:TQ, :DA_VDIM] / acc_ref[:TQ, DA_VDIM:]
         - lam * (acc_ref[TQ:, :DA_VDIM] / acc_ref[TQ:, DA_VDIM:]))
    ms = jnp.mean(o * o, axis=-1, keepdims=True)
    o = (o * lax.rsqrt(ms + EPS)) * g_ref[...] * (1.0 - LAM_INIT)
    o_ref[0] = o.astype(o_ref.dtype)


def _diff_attention(dac, lam_params, qa, ka, va, bias_tiles, subln_g):
    B, S, _ = qa.shape
    T = T_ATT
    TQ = TQ_ATT
    return pl.pallas_call(
        _diffattn_kernel,
        grid=(B, DA_HEADS, S // TQ),
        in_specs=[pl.BlockSpec(memory_space=pltpu.SMEM),
                  pl.BlockSpec((4, HEAD_DIM), lambda b, h, i: (0, 0)),
                  pl.BlockSpec((1, TQ, LANES), lambda b, h, i: (b, i, h)),
                  pl.BlockSpec((1, S, LANES), lambda b, h, i: (b, 0, h)),
                  pl.BlockSpec((1, S, LANES), lambda b, h, i: (b, 0, h)),
                  pl.BlockSpec((1, 5, T, T), lambda b, h, i: (h, 0, 0, 0)),
                  pl.BlockSpec((1, DA_VDIM), lambda b, h, i: (0, 0))],
        out_specs=pl.BlockSpec((1, TQ, LANES), lambda b, h, i: (b, i, h)),
        out_shape=jax.ShapeDtypeStruct((B, S, DA_V), BF16),
        scratch_shapes=[pltpu.VMEM((S, 2 * DA_VDIM), BF16), pltpu.VMEM((2 * TQ, 2 * DA_VDIM), F32)],
        compiler_params=_cparams(("arbitrary",) * 3, 56),
        name="diff_attention",
    )(dac, lam_params, qa, ka, va, bias_tiles, subln_g)


def _dilattn_kernel(g, dilc_ref, q_ref, k_ref, v_ref, bias_ref, o_ref, lse_ref):
    TQ = TQ_DIL
    L = k_ref.shape[2]
    tiles_per_step = q_ref.shape[2] // TQ
    n_q = L // TQ
    low = lax.broadcasted_iota(jnp.int32, (TQ, LANES), 1) < HEAD_DIM
    ones = jnp.ones((W_DIL, LANES), BF16)

    def run(use_max):
        for sub in range(tiles_per_step):
            for hp in range(DIL_HEADS // 2):
                one_tile(use_max, sub, hp)

    def one_tile(use_max, sub, hp):
            i = pl.program_id(2) * tiles_per_step + sub
            w0 = pl.multiple_of(jnp.clip(i * TQ - HALF_DIL, 0, L - W_DIL), HALF_DIL)
            variant = jnp.where(i == 0, 0, jnp.where(i == n_q - 1, 2, 1))
            cs = slice(hp * LANES, (hp + 1) * LANES)
            qp = q_ref[0, 0, sub * TQ:(sub + 1) * TQ, cs]
            kp = k_ref[0, 0, pl.ds(w0, W_DIL), cs]
            vext = jnp.concatenate([v_ref[0, 0, pl.ds(w0, W_DIL), cs], ones], axis=1)
            zero = jnp.zeros_like(qp)
            qs = jnp.concatenate([jnp.where(low, qp, zero), jnp.where(low, zero, qp)], axis=0)
            u = lax.dot_general(qs, kp, (((1,), (1,)), ((), ())), preferred_element_type=F32)
            u = u + jnp.concatenate([bias_ref[0, variant, 2 * hp], bias_ref[0, variant, 2 * hp + 1]], axis=0)
            if use_max:
                m = jnp.max(u, axis=-1, keepdims=True)
                u = u - m
            r = jnp.dot(jnp.exp2(u).astype(BF16), vext, preferred_element_type=F32)
            num, den = r[:, :LANES], r[:, LANES:]
            lse2 = jnp.log2(den)
            if use_max:
                lse2 = lse2 + m
            o_pair = jnp.where(low, num[:TQ] / den[:TQ], num[TQ:] / den[TQ:]).astype(o_ref.dtype)
            lse_pair = LN2 * jnp.where(low, lse2[:TQ] + dilc_ref[g, 2 * hp, 0], lse2[TQ:] + dilc_ref[g, 2 * hp + 1, 0])
            if len(o_ref.shape) == 7:
                n_grp = o_ref.shape[3]
                for t in range(TQ // (n_grp * SUBLANES)):
                    for grp in range(n_grp):
                        lo = (t * n_grp + grp) * SUBLANES
                        o_ref[0, sub * (TQ // (n_grp * SUBLANES)) + t, hp, grp, 0] = o_pair[lo:lo + SUBLANES]
                        lse_ref[0, sub * (TQ // (n_grp * SUBLANES)) + t, hp, grp, 0] = lse_pair[lo:lo + SUBLANES]
            else:
                o_ref[0, 0, sub * TQ:(sub + 1) * TQ, cs] = o_pair
                lse_ref[0, 0, sub * TQ:(sub + 1) * TQ, cs] = lse_pair

    fast_ok = dilc_ref[g, 0, 1] > 0.5
    pl.when(fast_ok)(functools.partial(run, False))
    pl.when(jnp.logical_not(fast_ok))(functools.partial(run, True))


def _dilated_attention(dilc, dil, bias_tiles, g):
    B, r, L, _ = dil.shape
    kv_mode = pl.Buffered(1) if L * DIL_W * 2 > (4 << 20) else pl.Buffered(2)
    qmap = lambda b, c, i: (b, c, i, 0)
    rows = min(ROWS_DIL, L)
    if r == 1:
        o_spec = pl.BlockSpec((1, 1, rows, DIL_W), qmap)
        o_shape, o_dtype = (B, r, L, DIL_W), BF16
    else:
        n = TM_OUT // r
        nk = DIL_W // LANES
        o_spec = pl.BlockSpec((1, rows // n, nk, n // SUBLANES, 1, SUBLANES, LANES),
                              lambda b, c, i: (b, i, 0, 0, c, 0, 0))
        o_shape, o_dtype = (B, L // n, nk, n // SUBLANES, r, SUBLANES, LANES), F32
    return pl.pallas_call(
        functools.partial(_dilattn_kernel, g),
        grid=(B, r, L // rows),
        in_specs=[pl.BlockSpec(memory_space=pltpu.SMEM),
                  pl.BlockSpec((1, 1, rows, DIL_W), qmap),
                  pl.BlockSpec((1, 1, L, DIL_W), lambda b, c, i: (b, c, 0, 1), pipeline_mode=kv_mode),
                  pl.BlockSpec((1, 1, L, DIL_W), lambda b, c, i: (b, c, 0, 2), pipeline_mode=kv_mode),
                  pl.BlockSpec((1, 3, DIL_HEADS, TQ_DIL, W_DIL), lambda b, c, i: (g, 0, 0, 0, 0))],
        out_specs=(o_spec, o_spec),
        out_shape=(jax.ShapeDtypeStruct(o_shape, o_dtype), jax.ShapeDtypeStruct(o_shape, F32)),
        compiler_params=_cparams(("arbitrary",) * 3, 48),
        name=f"dilated_attention_r{r}",
    )(dilc, dil, dil, dil, bias_tiles)


def _interleave_classes(ref, r, k, il_ref):
    nk = DIL_W // LANES
    n = ref.shape[2] // (r * nk)
    pieces = [ref[0, 0, pl.ds(k * n * r + (l // SUBLANES) * r * SUBLANES + l % SUBLANES, r, stride=SUBLANES), :]
              for l in range(n)]
    if r % 8 == 0:
        return jnp.concatenate(pieces, axis=0)
    for l in range(n):
        il_ref[pl.ds(l * r, r), :] = pieces[l]
    return il_ref[...]


def _merge_mlp_kernel(x_ref, mod_ref, ya_ref, o0_ref, l0_ref, o1_ref, l1_ref, o2_ref, l2_ref, gate_ref,
                      wa_ref, wb_ref, wo_ref, g2_ref, wu_ref, wd_ref, out_ref, il_ref, yb_ref):
    mod = mod_ref[0]

    @pl.when(pl.program_id(0) == 0)
    def _():
        yb_ref[...] = jnp.zeros(yb_ref.shape, BF16)

    yb = yb_ref[...]
    r1, r2 = DIL_CONFIG[1][1], DIL_CONFIG[2][1]

    def combine_chunk(k):
        cs = slice(k * LANES, (k + 1) * LANES)
        outs = (o0_ref[0, :, cs].astype(F32), _interleave_classes(o1_ref, r1, k, il_ref.at[0, k]),
                _interleave_classes(o2_ref, r2, k, il_ref.at[0, k]))
        lses = (l0_ref[0, :, cs], _interleave_classes(l1_ref, r1, k, il_ref.at[1, k]),
                _interleave_classes(l2_ref, r2, k, il_ref.at[1, k]))
        mx = jnp.maximum(jnp.maximum(lses[0], lses[1]), lses[2])
        es = [jnp.exp(l - mx) for l in lses]
        den = es[0] + es[1] + es[2]
        yb_ref[:, cs] = ((es[0] * outs[0] + es[1] * outs[1] + es[2] * outs[2]) / den).astype(BF16)

    pa = jnp.dot(ya_ref[0], wa_ref[...], preferred_element_type=F32)
    pb = jnp.dot(yb, wb_ref[...], preferred_element_type=F32)
    gates = gate_ref[0]
    merged = gates[:, :D_MODEL].astype(F32) * pa + gates[:, D_MODEL:].astype(F32) * pb
    x = x_ref[0] + mod[2:3] * jnp.dot(merged.astype(BF16), wo_ref[...], preferred_element_type=F32)

    ms = jnp.mean(x * x, axis=-1, keepdims=True)
    h = (x * lax.rsqrt(ms + EPS)) * g2_ref[...]
    hb = (h * (1.0 + mod[4:5]) + mod[3:4]).astype(BF16)
    acc = jnp.zeros(x.shape, F32)
    for f in range(D_FF // D_MODEL):
        cs = slice(f * D_MODEL, (f + 1) * D_MODEL)
        u = jnp.maximum(jnp.dot(hb, wu_ref[:, cs], preferred_element_type=F32), 0.0)
        combine_chunk(f)
        acc = acc + jnp.dot((u * u).astype(BF16), wd_ref[cs, :], preferred_element_type=F32)
    out_ref[0] = x + mod[5:6] * acc


def _merge_mlp(x, mod, ya, o0, l0, o1, l1, o2, l2, gates, wa, wb, wo, g2, wu, wd):
    B, S, _ = x.shape
    tm = TM_OUT
    n = S // tm
    const = lambda t: (0, 0)
    prev = lambda t: jnp.maximum(t - 1, 0)
    cur = lambda t: jnp.minimum(t, B * n - 1)
    tspec = lambda w: pl.BlockSpec((1, tm, w), lambda t: (prev(t) // n, prev(t) % n, 0))
    dspec = lambda w: pl.BlockSpec((1, tm, w), lambda t: (cur(t) // n, cur(t) % n, 0))
    cspec = lambda a: pl.BlockSpec((1, 1) + a.shape[2:], lambda t: (cur(t) // n, cur(t) % n, 0, 0))
    wspec = lambda a: pl.BlockSpec(a.shape, const, pipeline_mode=pl.Buffered(1))
    return pl.pallas_call(
        _merge_mlp_kernel,
        grid=(B * n + 1,),
        in_specs=[tspec(D_MODEL), pl.BlockSpec((1, 6, D_MODEL), lambda t: (prev(t) // n, 0, 0)), tspec(DA_V),
                  dspec(DIL_W), dspec(DIL_W), cspec(o1), cspec(l1), cspec(o2), cspec(l2),
                  tspec(2 * D_MODEL), wspec(wa), wspec(wb), wspec(wo),
                  pl.BlockSpec((1, D_MODEL), const), wspec(wu), wspec(wd)],
        out_specs=tspec(D_MODEL),
        out_shape=jax.ShapeDtypeStruct((B, S, D_MODEL), F32),
        scratch_shapes=[pltpu.VMEM((2, DIL_W // LANES, tm, LANES), F32), pltpu.VMEM((tm, DIL_W), BF16)],
        compiler_params=_cparams(("arbitrary",), 56),
        name="merge_mlp",
    )(x, mod, ya, o0, l0, o1, l1, o2, l2, gates, wa, wb, wo, g2, wu, wd)


def _pair_heads(a, b):
    d = a.shape[0]
    return jnp.stack([a.reshape(d, DA_HEADS, HEAD_DIM), b.reshape(d, DA_HEADS, HEAD_DIM)],
                     axis=2).reshape(d, 2 * DA_QK)


def _layer(x, mod, p):
    B, S, _ = x.shape
    qa, ka, va, dil0, dil1, dil2, gates = _inproj(x, mod, p["g1"], p["w_in"], p["gains"], p["gmat"])
    ya = _diff_attention(p["dac"], p["lam"], qa, ka, va, p["da_bias"], p["subln_g"])
    o0, l0 = _dilated_attention(p["dilc"], dil0.reshape(B, 1, S, 3 * DIL_W), p["dil_bias"], 0)
    o1, l1 = _dilated_attention(p["dilc"], dil1, p["dil_bias"], 1)
    o2, l2 = _dilated_attention(p["dilc"], dil2, p["dil_bias"], 2)
    rows = lambda a: a.reshape(B, a.shape[1], -1, LANES)
    return _merge_mlp(x, mod, ya, o0.reshape(B, S, DIL_W), l0.reshape(B, S, DIL_W),
                      rows(o1), rows(l1), rows(o2), rows(l2), gates,
                      p["w_br_a"], p["w_br_b"], p["w_o"], p["g2"], p["w_up"], p["w_down"])


def kernel(x_prompt, x_sample, c_prompt, c_sample, rel_bias, norm1_g, w_ada, b_ada, w_in, qn_a, kn_a,
           lambda_q1, lambda_k1, lambda_q2, lambda_k2, subln_g, qn_b, kn_b, w_br_a, w_br_b, w_o,
           norm2_g, w_up, w_down):
    nbp, nbs = c_prompt.shape[0], c_sample.shape[0]
    pad = (-(nbp + nbs)) % 8
    c_all = jnp.concatenate([c_prompt, c_sample, jnp.zeros((pad, D_MODEL), F32)], axis=0)
    mod = _modulation(c_all, w_ada[0], b_ada[0]).reshape(-1, 6, D_MODEL)

    w = w_in[0]
    q1, q2, k1, k2 = (w[:, n * DA_QK:(n + 1) * DA_QK] for n in range(4))
    w_perm = jnp.concatenate([_pair_heads(q1, q2), _pair_heads(k1, k2), w[:, 4 * DA_QK:]], axis=1)

    scale = HEAD_DIM ** -0.5
    tile8 = lambda v: jnp.tile(v, DIL_W // HEAD_DIM)
    gain_rows = [tile8(qn_a[0]) * (scale * LOG2E)] * 2 + [tile8(kn_a[0])] * 2
    for g in range(N_DIL):
        gain_rows += [tile8(qn_b[0, g]) * (scale * LOG2E), tile8(kn_b[0, g])]
    ids = jnp.arange(NORM_W) // HEAD_DIM
    gmat = jnp.where(ids[:, None] == ids[None, :], 1.0 / HEAD_DIM, 0.0).astype(BF16)

    def logit_reference(q_gain, k_gain, tab):
        qk_bound = HEAD_DIM * scale * jnp.max(jnp.abs(q_gain)) * jnp.max(jnp.abs(k_gain)) * BF16_ROUND_MARGIN
        c_ref = LOG2E * (qk_bound + jnp.max(tab, axis=0)) - DA_HEADROOM
        u_min = LOG2E * (jnp.min(tab, axis=0) - qk_bound) - c_ref
        return c_ref, jnp.all(u_min >= DA_MIN_EXP).astype(F32) * jnp.ones_like(c_ref)

    da_tab = rel_bias[:, :DA_HEADS]
    c_ref, da_ok = logit_reference(qn_a[0], kn_a[0], da_tab)
    dac = jnp.stack([c_ref, c_ref - LOG2E * da_tab[N_BUCKETS // 2 - 1], c_ref - LOG2E * da_tab[N_BUCKETS - 1],
                     da_ok], axis=1)
    dilc = jnp.stack([jnp.stack(logit_reference(
        qn_b[0, g], kn_b[0, g], rel_bias[:, DA_HEADS + g * DIL_HEADS:DA_HEADS + (g + 1) * DIL_HEADS]), axis=1)
        for g in range(N_DIL)], axis=0)

    p = {
        "g1": norm1_g[0].reshape(1, D_MODEL),
        "g2": norm2_g[0].reshape(1, D_MODEL),
        "w_in": w_perm.astype(BF16),
        "gains": jnp.stack(gain_rows, axis=0),
        "gmat": gmat,
        "lam": jnp.stack([lambda_q1[0], lambda_k1[0], lambda_q2[0], lambda_k2[0]], axis=0),
        "subln_g": subln_g[0].reshape(1, DA_VDIM),
        "dac": dac,
        "da_bias": _da_bias_tiles(rel_bias, dac),
        "dilc": dilc,
        "dil_bias": _dil_bias_tiles(rel_bias, dilc),
        "w_br_a": w_br_a[0].astype(BF16),
        "w_br_b": w_br_b[0].astype(BF16),
        "w_o": w_o[0].astype(BF16),
        "w_up": w_up[0].astype(BF16),
        "w_down": w_down[0].astype(BF16),
    }
    y_prompt = _layer(x_prompt, mod[:nbp], p)
    y_sample = _layer(x_sample, mod[nbp:nbp + nbs], p)
    return (y_prompt, y_sample)
```

```python
import functools
import math

import jax
import jax.numpy as jnp
from jax import lax
from jax.experimental import pallas as pl
from jax.experimental.pallas import tpu as pltpu

F32 = jnp.float32
BF16 = jnp.bfloat16

D_MODEL = 1024
HEAD_DIM = 64
DA_HEADS = 8
DA_VDIM = 2 * HEAD_DIM
DIL_CONFIG = ((128, 1), (512, 4), (2048, 16))
N_DIL = 3
DIL_HEADS = 8
D_FF = 4 * D_MODEL
N_BUCKETS = 32
NEG_INF = -1e30
EPS = 1e-6
LAM_INIT = 0.8 - 0.6 * math.exp(-0.3 * 0)

DA_QK = DA_HEADS * HEAD_DIM
DA_V = DA_HEADS * DA_VDIM
DIL_W = DIL_HEADS * HEAD_DIM
DIL_COLS = 3 * N_DIL * DIL_W
IN_COLS = 4 * DA_QK + DA_V + DIL_COLS + 2 * D_MODEL

LANES = 128
SUBLANES = 8
COL_BLK = 512
N_COL_BLKS = IN_COLS // COL_BLK
NORM_W = 256

TM_PROJ = 512
TM_OUT = 256
T_ATT = 512
TQ_ATT = 1024
TQ_DIL = 128
ROWS_DIL = 1024
HALF_DIL = 64
W_DIL = TQ_DIL + 2 * HALF_DIL

LOG2E = math.log2(math.e)
LN2 = math.log(2.0)
DA_HEADROOM = 64.0
DA_MIN_EXP = -60.0
BF16_ROUND_MARGIN = 1.02

_MAX_EXACT = N_BUCKETS // 4
_BUCKET_STEPS = (12, 16, 23, 32, 46, 64, 91)


def _cparams(sem, vmem_mb):
    return pltpu.CompilerParams(dimension_semantics=sem, vmem_limit_bytes=vmem_mb * 1024 * 1024)


def _mod_kernel(c_ref, w_ref, b_ref, o_ref):
    c = c_ref[...]
    a = c * (1.0 / (1.0 + jnp.exp(-c)))
    o_ref[...] = jnp.dot(a, w_ref[...], preferred_element_type=F32,
                         precision=lax.Precision.HIGHEST) + b_ref[...]


def _modulation(c_all, w_ada, b_ada):
    rows = c_all.shape[0]
    n = w_ada.shape[1]
    return pl.pallas_call(
        _mod_kernel,
        grid=(n // D_MODEL,),
        in_specs=[pl.BlockSpec((rows, D_MODEL), lambda j: (0, 0)),
                  pl.BlockSpec((D_MODEL, D_MODEL), lambda j: (0, j)),
                  pl.BlockSpec((1, D_MODEL), lambda j: (0, j))],
        out_specs=pl.BlockSpec((rows, D_MODEL), lambda j: (0, j)),
        out_shape=jax.ShapeDtypeStruct((rows, n), F32),
        compiler_params=_cparams(("arbitrary",), 32),
        name="modulation",
    )(c_all, w_ada, b_ada.reshape(1, n))


def _rel_bucket(rel):
    n = jnp.abs(rel)
    large = jnp.full(rel.shape, _MAX_EXACT, jnp.int32)
    for t in _BUCKET_STEPS:
        large = large + (n >= t).astype(jnp.int32)
    return jnp.where(n < _MAX_EXACT, n, large) + jnp.where(rel > 0, N_BUCKETS // 2, 0)


def _table_values(bucket, tab_ref, col):
    val = jnp.zeros(bucket.shape, F32)
    for b in range(N_BUCKETS):
        val = jnp.where(bucket == b, tab_ref[b, col], val)
    return val


def _bias_lookup(rel, tab_ref, col):
    return _table_values(_rel_bucket(rel), tab_ref, col)


def _da_bias_kernel(tab_ref, dac_ref, o_ref):
    h = pl.program_id(0)
    sat = _BUCKET_STEPS[-1]
    row = lax.broadcasted_iota(jnp.int32, (LANES, LANES), 0)
    col = lax.broadcasted_iota(jnp.int32, (LANES, LANES), 1)
    far = (jnp.full((LANES, LANES), -dac_ref[h, 1], F32), jnp.full((LANES, LANES), -dac_ref[h, 2], F32))
    for d in range(-2, 3):
        for br in range(T_ATT // LANES):
            for bc in range(T_ATT // LANES):
                base = d * T_ATT + (bc - br) * LANES
                if base + LANES - 1 <= -sat:
                    val = far[0]
                elif base - (LANES - 1) >= sat:
                    val = far[1]
                else:
                    val = LOG2E * _bias_lookup(col - row + base, tab_ref, h) - dac_ref[h, 0]
                o_ref[0, d + 2, br * LANES:(br + 1) * LANES, bc * LANES:(bc + 1) * LANES] = val


def _da_bias_tiles(rel_bias, dac):
    assert T_ATT >= _BUCKET_STEPS[-1]
    return pl.pallas_call(
        _da_bias_kernel,
        grid=(DA_HEADS,),
        in_specs=[pl.BlockSpec(memory_space=pltpu.SMEM), pl.BlockSpec(memory_space=pltpu.SMEM)],
        out_specs=pl.BlockSpec((1, 5, T_ATT, T_ATT), lambda h: (h, 0, 0, 0)),
        out_shape=jax.ShapeDtypeStruct((DA_HEADS, 5, T_ATT, T_ATT), F32),
        compiler_params=_cparams(("arbitrary",), 32),
        name="da_bias_tiles",
    )(rel_bias, dac)


def _dil_bias_kernel(tab_ref, dilc_ref, o_ref):
    g = pl.program_id(0)
    v = pl.program_id(1)
    row = lax.broadcasted_iota(jnp.int32, (TQ_DIL, W_DIL), 0)
    col = lax.broadcasted_iota(jnp.int32, (TQ_DIL, W_DIL), 1)
    rel = col - row - HALF_DIL * v
    dilation = lax.shift_left(jnp.int32(1), 2 * g)
    bucket = _rel_bucket(rel * dilation)
    for h in range(DIL_HEADS):
        val = LOG2E * _table_values(bucket, tab_ref, DA_HEADS + g * DIL_HEADS + h) - dilc_ref[g, h, 0]
        o_ref[0, 0, h] = jnp.where(jnp.abs(rel) <= HALF_DIL, val, NEG_INF)


def _dil_bias_tiles(rel_bias, dilc):
    return pl.pallas_call(
        _dil_bias_kernel,
        grid=(N_DIL, 3),
        in_specs=[pl.BlockSpec(memory_space=pltpu.SMEM), pl.BlockSpec(memory_space=pltpu.SMEM)],
        out_specs=pl.BlockSpec((1, 1, DIL_HEADS, TQ_DIL, W_DIL), lambda g, v: (g, v, 0, 0, 0)),
        out_shape=jax.ShapeDtypeStruct((N_DIL, 3, DIL_HEADS, TQ_DIL, W_DIL), F32),
        compiler_params=_cparams(("arbitrary",) * 2, 32),
        name="dil_bias_tiles",
    )(rel_bias, dilc)


_DIL_BLK0 = (4 * DA_QK + DA_V) // COL_BLK
_GATE_BLK0 = _DIL_BLK0 + DIL_COLS // COL_BLK
_QK_BLOCKS = {0: 0, 1: 1, 2: 2, 3: 3}
for _g in range(N_DIL):
    _QK_BLOCKS[_DIL_BLK0 + 3 * _g] = 4 + 2 * _g
    _QK_BLOCKS[_DIL_BLK0 + 3 * _g + 1] = 5 + 2 * _g


def _inproj_kernel(x_ref, mod_ref, g1_ref, w_ref, gain_ref, gmat_ref,
                   qa_ref, ka_ref, va_ref, dil0_ref, dil1_ref, dil2_ref, gate_ref, cls_ref):
    tm = x_ref.shape[1]
    dil_refs = (dil0_ref, dil1_ref, dil2_ref)
    x = x_ref[0]
    mod = mod_ref[0]
    ms = jnp.mean(x * x, axis=-1, keepdims=True)
    h = (x * lax.rsqrt(ms + EPS)) * g1_ref[...]
    h = h * (1.0 + mod[1:2]) + mod[0:1]
    gmat = gmat_ref[...]

    for k in range(D_MODEL // LANES):
        cls_ref[k] = h[:, k * LANES:(k + 1) * LANES]

    def class_major(r):
        chunks = [jnp.concatenate([cls_ref[k, pl.ds(c, tm // r, stride=r), :] for c in range(r)], axis=0)
                  for k in range(D_MODEL // LANES)]
        return jnp.concatenate(chunks, axis=1).astype(BF16)

    hb_by_dilation = {1: h.astype(BF16)}
    for _, r in DIL_CONFIG:
        if r not in hb_by_dilation:
            hb_by_dilation[r] = class_major(r)

    order = sorted(range(N_COL_BLKS), key=lambda b: (0 if b >= _GATE_BLK0 else 1 if b in _QK_BLOCKS else 2, b))
    for blk in order:
        dilation = 1
        if blk < _DIL_BLK0:
            dst, off = (qa_ref, ka_ref, va_ref)[blk // 2], (blk % 2) * COL_BLK
        elif blk < _GATE_BLK0:
            g = (blk - _DIL_BLK0) // 3
            dst, off, dilation = dil_refs[g], ((blk - _DIL_BLK0) % 3) * COL_BLK, DIL_CONFIG[g][1]
        else:
            dst, off = gate_ref, (blk - _GATE_BLK0) * COL_BLK
        acc = jnp.dot(hb_by_dilation[dilation], w_ref[:, blk * COL_BLK:(blk + 1) * COL_BLK],
                      preferred_element_type=F32)

        def put(lo, y):
            cols = slice(off + lo, off + lo + y.shape[1])
            if dilation == 1:
                dst[0, :, cols] = y.astype(BF16)
            else:
                n = tm // dilation
                for c in range(dilation):
                    dst[0, c, :, cols] = y[c * n:(c + 1) * n].astype(BF16)

        if blk in _QK_BLOCKS:
            gi = _QK_BLOCKS[blk]
            for half in range(COL_BLK // NORM_W):
                lo = half * NORM_W
                a = acc[:, lo:lo + NORM_W]
                hm = jnp.dot((a * a).astype(BF16), gmat, preferred_element_type=F32)
                put(lo, (a * lax.rsqrt(hm + EPS)) * gain_ref[gi:gi + 1, lo:lo + NORM_W])
        elif blk >= _GATE_BLK0:
            put(0, 1.0 / (1.0 + jnp.exp(-acc)))
        else:
            put(0, acc)


def _inproj(x, mod, g1, w_in_b, gains, gmat):
    B, S, _ = x.shape
    tm = TM_PROJ
    const = lambda b, i: (0, 0)
    tok = lambda b, i: (b, i, 0)
    dil_shapes = tuple(jax.ShapeDtypeStruct((B, S, 3 * DIL_W) if r == 1 else (B, r, S // r, 3 * DIL_W), BF16)
                       for _, r in DIL_CONFIG)
    dil_specs = tuple(pl.BlockSpec((1, tm, 3 * DIL_W), tok) if r == 1 else
                      pl.BlockSpec((1, r, tm // r, 3 * DIL_W), lambda b, i: (b, 0, i, 0))
                      for _, r in DIL_CONFIG)
    tok_shapes = (jax.ShapeDtypeStruct((B, S, 2 * DA_QK), BF16),
                  jax.ShapeDtypeStruct((B, S, 2 * DA_QK), BF16),
                  jax.ShapeDtypeStruct((B, S, DA_V), BF16))
    gate_shape = jax.ShapeDtypeStruct((B, S, 2 * D_MODEL), BF16)
    tok_spec = lambda s: pl.BlockSpec((1, tm, s.shape[-1]), tok)
    return pl.pallas_call(
        _inproj_kernel,
        grid=(B, S // tm),
        in_specs=[pl.BlockSpec((1, tm, D_MODEL), tok),
                  pl.BlockSpec((1, 6, D_MODEL), lambda b, i: (b, 0, 0)),
                  pl.BlockSpec((1, D_MODEL), const),
                  pl.BlockSpec((D_MODEL, IN_COLS), const, pipeline_mode=pl.Buffered(1)),
                  pl.BlockSpec(gains.shape, const),
                  pl.BlockSpec((NORM_W, NORM_W), const)],
        out_specs=tuple(map(tok_spec, tok_shapes)) + dil_specs + (tok_spec(gate_shape),),
        out_shape=tok_shapes + dil_shapes + (gate_shape,),
        scratch_shapes=[pltpu.VMEM((D_MODEL // LANES, tm, LANES), F32)],
        compiler_params=_cparams(("arbitrary", "arbitrary"), 56),
        name="in_projection",
    )(x, mod, g1, w_in_b, gains, gmat)


def _diffattn_kernel(dac_ref, lam_ref, q_ref, k_ref, v_ref, bias_ref, g_ref, o_ref, vext_ref, acc_ref):
    T = T_ATT
    TQ = TQ_ATT
    n_sub = TQ // T
    h = pl.program_id(1)
    S = k_ref.shape[1]
    n_kv = S // T

    vext_ref[:, :DA_VDIM] = v_ref[0]
    vext_ref[:, DA_VDIM:] = jnp.ones((S, DA_VDIM), BF16)

    lp = lam_ref[...]
    lam = (jnp.exp(jnp.sum(lp[0:1] * lp[1:2], axis=-1, keepdims=True))
           - jnp.exp(jnp.sum(lp[2:3] * lp[3:4], axis=-1, keepdims=True)) + LAM_INIT)

    def kv_rows(j):
        return slice(j * T, (j + 1) * T) if isinstance(j, int) else pl.ds(pl.multiple_of(j * T, T), T)

    def q_tile(use_max, i, carry):
        q_rows = pl.ds(pl.multiple_of(i * TQ, TQ), TQ)
        q = q_ref[0, q_rows, :]
        lane = lax.broadcasted_iota(jnp.int32, q.shape, 1)
        zero = jnp.zeros_like(q)
        qs = jnp.concatenate([jnp.where(lane < HEAD_DIM, q, zero), jnp.where(lane >= HEAD_DIM, q, zero)], axis=0)

        def logits(j):
            s = lax.dot_general(qs, k_ref[0, kv_rows(j), :], (((1,), (1,)), ((), ())),
                                preferred_element_type=F32)
            tiles = [bias_ref[0, jnp.clip(j - (i * n_sub + t), -2, 2) + 2] for t in range(n_sub)]
            return s + jnp.concatenate(tiles + tiles, axis=0)

        if not use_max:
            for j in range(n_kv):
                pv = jnp.dot(jnp.exp2(logits(j)).astype(BF16), vext_ref[kv_rows(j), :],
                             preferred_element_type=F32)
                if j == 0:
                    acc_ref[...] = pv
                else:
                    acc_ref[...] += pv
        else:
            def body(j, mc):
                m, acc = mc
                u = logits(j)
                m_new = jnp.maximum(m, jnp.max(u, axis=-1, keepdims=True))
                p = jnp.exp2(u - m_new)
                acc = jnp.exp2(m - m_new) * acc + jnp.dot(p.astype(BF16), vext_ref[kv_rows(j), :],
                                                           preferred_element_type=F32)
                return m_new, acc

            init = (jnp.full((2 * TQ, 1), NEG_INF, F32), jnp.zeros(acc_ref.shape, F32))
            acc_ref[...] = lax.fori_loop(0, n_kv, body, init)[1]

        o = (acc_ref[:TQ, :DA_VDIM] / acc_ref[:TQ, DA_VDIM:]
             - lam * (acc_ref[TQ:, :DA_VDIM] / acc_ref[TQ:, DA_VDIM:]))
        ms = jnp.mean(o * o, axis=-1, keepdims=True)
        o = (o * lax.rsqrt(ms + EPS)) * g_ref[...] * (1.0 - LAM_INIT)
        o_ref[0, q_rows, :] = o.astype(o_ref.dtype)
        return carry

    def all_q_tiles(use_max):
        lax.fori_loop(0, S // TQ, functools.partial(q_tile, use_max), 0)

    fast_ok = dac_ref[h, 3] > 0.5
    pl.when(fast_ok)(functools.partial(all_q_tiles, False))
    pl.when(jnp.logical_not(fast_ok))(functools.partial(all_q_tiles, True))


def _diff_attention(dac, lam_params, qa, ka, va, bias_tiles, subln_g):
    B, S, _ = qa.shape
    T = T_ATT
    TQ = TQ_ATT
    return pl.pallas_call(
        _diffattn_kernel,
        grid=(B, DA_HEADS),
        in_specs=[pl.BlockSpec(memory_space=pltpu.SMEM),
                  pl.BlockSpec((4, HEAD_DIM), lambda b, h: (0, 0)),
                  pl.BlockSpec((1, S, LANES), lambda b, h: (b, 0, h)),
                  pl.BlockSpec((1, S, LANES), lambda b, h: (b, 0, h)),
                  pl.BlockSpec((1, S, LANES), lambda b, h: (b, 0, h)),
                  pl.BlockSpec((1, 5, T, T), lambda b, h: (h, 0, 0, 0)),
                  pl.BlockSpec((1, DA_VDIM), lambda b, h: (0, 0))],
        out_specs=pl.BlockSpec((1, S, LANES), lambda b, h: (b, 0, h)),
        out_shape=jax.ShapeDtypeStruct((B, S, DA_V), BF16),
        scratch_shapes=[pltpu.VMEM((S, 2 * DA_VDIM), BF16), pltpu.VMEM((2 * TQ, 2 * DA_VDIM), F32)],
        compiler_params=_cparams(("arbitrary",) * 2, 56),
        name="diff_attention",
    )(dac, lam_params, qa, ka, va, bias_tiles, subln_g)


def _dilattn_kernel(g, dilc_ref, q_ref, k_ref, v_ref, bias_ref, o_ref, lse_ref):
    TQ = TQ_DIL
    L = k_ref.shape[2]
    tiles_per_step = q_ref.shape[2] // TQ
    n_q = L // TQ
    low = lax.broadcasted_iota(jnp.int32, (TQ, LANES), 1) < HEAD_DIM
    ones = jnp.ones((W_DIL, LANES), BF16)

    def run(use_max):
        for sub in range(tiles_per_step):
            for hp in range(DIL_HEADS // 2):
                one_tile(use_max, sub, hp)

    def one_tile(use_max, sub, hp):
            i = pl.program_id(2) * tiles_per_step + sub
            w0 = pl.multiple_of(jnp.clip(i * TQ - HALF_DIL, 0, L - W_DIL), HALF_DIL)
            variant = jnp.where(i == 0, 0, jnp.where(i == n_q - 1, 2, 1))
            cs = slice(hp * LANES, (hp + 1) * LANES)
            qp = q_ref[0, 0, sub * TQ:(sub + 1) * TQ, cs]
            kp = k_ref[0, 0, pl.ds(w0, W_DIL), cs]
            vext = jnp.concatenate([v_ref[0, 0, pl.ds(w0, W_DIL), cs], ones], axis=1)
            zero = jnp.zeros_like(qp)
            qs = jnp.concatenate([jnp.where(low, qp, zero), jnp.where(low, zero, qp)], axis=0)
            u = lax.dot_general(qs, kp, (((1,), (1,)), ((), ())), preferred_element_type=F32)
            u = u + jnp.concatenate([bias_ref[0, variant, 2 * hp], bias_ref[0, variant, 2 * hp + 1]], axis=0)
            if use_max:
                m = jnp.max(u, axis=-1, keepdims=True)
                u = u - m
            r = jnp.dot(jnp.exp2(u).astype(BF16), vext, preferred_element_type=F32)
            num, den = r[:, :LANES], r[:, LANES:]
            lse2 = jnp.log2(den)
            if use_max:
                lse2 = lse2 + m
            o_pair = jnp.where(low, num[:TQ] / den[:TQ], num[TQ:] / den[TQ:]).astype(o_ref.dtype)
            lse_pair = LN2 * jnp.where(low, lse2[:TQ] + dilc_ref[g, 2 * hp, 0], lse2[TQ:] + dilc_ref[g, 2 * hp + 1, 0])
            if len(o_ref.shape) == 7:
                n_grp = o_ref.shape[3]
                for t in range(TQ // (n_grp * SUBLANES)):
                    for grp in range(n_grp):
                        lo = (t * n_grp + grp) * SUBLANES
                        o_ref[0, sub * (TQ // (n_grp * SUBLANES)) + t, hp, grp, 0] = o_pair[lo:lo + SUBLANES]
                        lse_ref[0, sub * (TQ // (n_grp * SUBLANES)) + t, hp, grp, 0] = lse_pair[lo:lo + SUBLANES]
            else:
                o_ref[0, 0, sub * TQ:(sub + 1) * TQ, cs] = o_pair
                lse_ref[0, 0, sub * TQ:(sub + 1) * TQ, cs] = lse_pair

    fast_ok = dilc_ref[g, 0, 1] > 0.5
    pl.when(fast_ok)(functools.partial(run, False))
    pl.when(jnp.logical_not(fast_ok))(functools.partial(run, True))


def _dilated_attention(dilc, dil, bias_tiles, g):
    B, r, L, _ = dil.shape
    kv_mode = pl.Buffered(1) if L * DIL_W * 2 > (4 << 20) else pl.Buffered(2)
    qmap = lambda b, c, i: (b, c, i, 0)
    rows = min(ROWS_DIL, L)
    if r == 1:
        o_spec = pl.BlockSpec((1, 1, rows, DIL_W), qmap)
        o_shape, o_dtype = (B, r, L, DIL_W), BF16
    else:
        n = TM_OUT // r
        nk = DIL_W // LANES
        o_spec = pl.BlockSpec((1, rows // n, nk, n // SUBLANES, 1, SUBLANES, LANES),
                              lambda b, c, i: (b, i, 0, 0, c, 0, 0))
        o_shape, o_dtype = (B, L // n, nk, n // SUBLANES, r, SUBLANES, LANES), F32
    return pl.pallas_call(
        functools.partial(_dilattn_kernel, g),
        grid=(B, r, L // rows),
        in_specs=[pl.BlockSpec(memory_space=pltpu.SMEM),
                  pl.BlockSpec((1, 1, rows, DIL_W), qmap),
                  pl.BlockSpec((1, 1, L, DIL_W), lambda b, c, i: (b, c, 0, 1), pipeline_mode=kv_mode),
                  pl.BlockSpec((1, 1, L, DIL_W), lambda b, c, i: (b, c, 0, 2), pipeline_mode=kv_mode),
                  pl.BlockSpec((1, 3, DIL_HEADS, TQ_DIL, W_DIL), lambda b, c, i: (g, 0, 0, 0, 0))],
        out_specs=(o_spec, o_spec),
        out_shape=(jax.ShapeDtypeStruct(o_shape, o_dtype), jax.ShapeDtypeStruct(o_shape, F32)),
        compiler_params=_cparams(("arbitrary",) * 3, 48),
        name=f"dilated_attention_r{r}",
    )(dilc, dil, dil, dil, bias_tiles)


def _interleave_classes(ref, r, k, il_ref):
    nk = DIL_W // LANES
    n = ref.shape[2] // (r * nk)
    pieces = [ref[0, 0, pl.ds(k * n * r + (l // SUBLANES) * r * SUBLANES + l % SUBLANES, r, stride=SUBLANES), :]
              for l in range(n)]
    if r % 8 == 0:
        return jnp.concatenate(pieces, axis=0)
    for l in range(n):
        il_ref[pl.ds(l * r, r), :] = pieces[l]
    return il_ref[...]


def _merge_mlp_kernel(x_ref, mod_ref, ya_ref, o0_ref, l0_ref, o1_ref, l1_ref, o2_ref, l2_ref, gate_ref,
                      wa_ref, wb_ref, wo_ref, g2_ref, wu_ref, wd_ref, out_ref, il_ref, yb_ref):
    mod = mod_ref[0]

    @pl.when(pl.program_id(0) == 0)
    def _():
        yb_ref[...] = jnp.zeros(yb_ref.shape, BF16)

    yb = yb_ref[...]
    r1, r2 = DIL_CONFIG[1][1], DIL_CONFIG[2][1]

    def combine_chunk(k):
        cs = slice(k * LANES, (k + 1) * LANES)
        outs = (o0_ref[0, :, cs].astype(F32), _interleave_classes(o1_ref, r1, k, il_ref.at[0, k]),
                _interleave_classes(o2_ref, r2, k, il_ref.at[0, k]))
        lses = (l0_ref[0, :, cs], _interleave_classes(l1_ref, r1, k, il_ref.at[1, k]),
                _interleave_classes(l2_ref, r2, k, il_ref.at[1, k]))
        mx = jnp.maximum(jnp.maximum(lses[0], lses[1]), lses[2])
        es = [jnp.exp(l - mx) for l in lses]
        den = es[0] + es[1] + es[2]
        yb_ref[:, cs] = ((es[0] * outs[0] + es[1] * outs[1] + es[2] * outs[2]) / den).astype(BF16)

    pa = jnp.dot(ya_ref[0], wa_ref[...], preferred_element_type=F32)
    pb = jnp.dot(yb, wb_ref[...], preferred_element_type=F32)
    gates = gate_ref[0]
    merged = gates[:, :D_MODEL].astype(F32) * pa + gates[:, D_MODEL:].astype(F32) * pb
    x = x_ref[0] + mod[2:3] * jnp.dot(merged.astype(BF16), wo_ref[...], preferred_element_type=F32)

    ms = jnp.mean(x * x, axis=-1, keepdims=True)
    h = (x * lax.rsqrt(ms + EPS)) * g2_ref[...]
    hb = (h * (1.0 + mod[4:5]) + mod[3:4]).astype(BF16)
    acc = jnp.zeros(x.shape, F32)
    for f in range(D_FF // D_MODEL):
        cs = slice(f * D_MODEL, (f + 1) * D_MODEL)
        u = jnp.maximum(jnp.dot(hb, wu_ref[:, cs], preferred_element_type=F32), 0.0)
        combine_chunk(f)
        acc = acc + jnp.dot((u * u).astype(BF16), wd_ref[cs, :], preferred_element_type=F32)
    out_ref[0] = x + mod[5:6] * acc


def _merge_mlp(x, mod, ya, o0, l0, o1, l1, o2, l2, gates, wa, wb, wo, g2, wu, wd):
    B, S, _ = x.shape
    tm = TM_OUT
    n = S // tm
    const = lambda t: (0, 0)
    prev = lambda t: jnp.maximum(t - 1, 0)
    cur = lambda t: jnp.minimum(t, B * n - 1)
    tspec = lambda w: pl.BlockSpec((1, tm, w), lambda t: (prev(t) // n, prev(t) % n, 0))
    dspec = lambda w: pl.BlockSpec((1, tm, w), lambda t: (cur(t) // n, cur(t) % n, 0))
    cspec = lambda a: pl.BlockSpec((1, 1) + a.shape[2:], lambda t: (cur(t) // n, cur(t) % n, 0, 0))
    wspec = lambda a: pl.BlockSpec(a.shape, const, pipeline_mode=pl.Buffered(1))
    return pl.pallas_call(
        _merge_mlp_kernel,
        grid=(B * n + 1,),
        in_specs=[tspec(D_MODEL), pl.BlockSpec((1, 6, D_MODEL), lambda t: (prev(t) // n, 0, 0)), tspec(DA_V),
                  dspec(DIL_W), dspec(DIL_W), cspec(o1), cspec(l1), cspec(o2), cspec(l2),
                  tspec(2 * D_MODEL), wspec(wa), wspec(wb), wspec(wo),
                  pl.BlockSpec((1, D_MODEL), const), wspec(wu), wspec(wd)],
        out_specs=tspec(D_MODEL),
        out_shape=jax.ShapeDtypeStruct((B, S, D_MODEL), F32),
        scratch_shapes=[pltpu.VMEM((2, DIL_W // LANES, tm, LANES), F32), pltpu.VMEM((tm, DIL_W), BF16)],
        compiler_params=_cparams(("arbitrary",), 56),
        name="merge_mlp",
    )(x, mod, ya, o0, l0, o1, l1, o2, l2, gates, wa, wb, wo, g2, wu, wd)


def _pair_heads(a, b):
    d = a.shape[0]
    return jnp.stack([a.reshape(d, DA_HEADS, HEAD_DIM), b.reshape(d, DA_HEADS, HEAD_DIM)],
                     axis=2).reshape(d, 2 * DA_QK)


def _layer(x, mod, p):
    B, S, _ = x.shape
    qa, ka, va, dil0, dil1, dil2, gates = _inproj(x, mod, p["g1"], p["w_in"], p["gains"], p["gmat"])
    ya = _diff_attention(p["dac"], p["lam"], qa, ka, va, p["da_bias"], p["subln_g"])
    o0, l0 = _dilated_attention(p["dilc"], dil0.reshape(B, 1, S, 3 * DIL_W), p["dil_bias"], 0)
    o1, l1 = _dilated_attention(p["dilc"], dil1, p["dil_bias"], 1)
    o2, l2 = _dilated_attention(p["dilc"], dil2, p["dil_bias"], 2)
    rows = lambda a: a.reshape(B, a.shape[1], -1, LANES)
    return _merge_mlp(x, mod, ya, o0.reshape(B, S, DIL_W), l0.reshape(B, S, DIL_W),
                      rows(o1), rows(l1), rows(o2), rows(l2), gates,
                      p["w_br_a"], p["w_br_b"], p["w_o"], p["g2"], p["w_up"], p["w_down"])


def kernel(x_prompt, x_sample, c_prompt, c_sample, rel_bias, norm1_g, w_ada, b_ada, w_in, qn_a, kn_a,
           lambda_q1, lambda_k1, lambda_q2, lambda_k2, subln_g, qn_b, kn_b, w_br_a, w_br_b, w_o,
           norm2_g, w_up, w_down):
    nbp, nbs = c_prompt.shape[0], c_sample.shape[0]
    pad = (-(nbp + nbs)) % 8
    c_all = jnp.concatenate([c_prompt, c_sample, jnp.zeros((pad, D_MODEL), F32)], axis=0)
    mod = _modulation(c_all, w_ada[0], b_ada[0]).reshape(-1, 6, D_MODEL)

    w = w_in[0]
    q1, q2, k1, k2 = (w[:, n * DA_QK:(n + 1) * DA_QK] for n in range(4))
    w_perm = jnp.concatenate([_pair_heads(q1, q2), _pair_heads(k1, k2), w[:, 4 * DA_QK:]], axis=1)

    scale = HEAD_DIM ** -0.5
    tile8 = lambda v: jnp.tile(v, DIL_W // HEAD_DIM)
    gain_rows = [tile8(qn_a[0]) * (scale * LOG2E)] * 2 + [tile8(kn_a[0])] * 2
    for g in range(N_DIL):
        gain_rows += [tile8(qn_b[0, g]) * (scale * LOG2E), tile8(kn_b[0, g])]
    ids = jnp.arange(NORM_W) // HEAD_DIM
    gmat = jnp.where(ids[:, None] == ids[None, :], 1.0 / HEAD_DIM, 0.0).astype(BF16)

    def logit_reference(q_gain, k_gain, tab):
        qk_bound = HEAD_DIM * scale * jnp.max(jnp.abs(q_gain)) * jnp.max(jnp.abs(k_gain)) * BF16_ROUND_MARGIN
        c_ref = LOG2E * (qk_bound + jnp.max(tab, axis=0)) - DA_HEADROOM
        u_min = LOG2E * (jnp.min(tab, axis=0) - qk_bound) - c_ref
        return c_ref, jnp.all(u_min >= DA_MIN_EXP).astype(F32) * jnp.ones_like(c_ref)

    da_tab = rel_bias[:, :DA_HEADS]
    c_ref, da_ok = logit_reference(qn_a[0], kn_a[0], da_tab)
    dac = jnp.stack([c_ref, c_ref - LOG2E * da_tab[N_BUCKETS // 2 - 1], c_ref - LOG2E * da_tab[N_BUCKETS - 1],
                     da_ok], axis=1)
    dilc = jnp.stack([jnp.stack(logit_reference(
        qn_b[0, g], kn_b[0, g], rel_bias[:, DA_HEADS + g * DIL_HEADS:DA_HEADS + (g + 1) * DIL_HEADS]), axis=1)
        for g in range(N_DIL)], axis=0)

    p = {
        "g1": norm1_g[0].reshape(1, D_MODEL),
        "g2": norm2_g[0].reshape(1, D_MODEL),
        "w_in": w_perm.astype(BF16),
        "gains": jnp.stack(gain_rows, axis=0),
        "gmat": gmat,
        "lam": jnp.stack([lambda_q1[0], lambda_k1[0], lambda_q2[0], lambda_k2[0]], axis=0),
        "subln_g": subln_g[0].reshape(1, DA_VDIM),
        "dac": dac,
        "da_bias": _da_bias_tiles(rel_bias, dac),
        "dilc": dilc,
        "dil_bias": _dil_bias_tiles(rel_bias, dilc),
        "w_br_a": w_br_a[0].astype(BF16),
        "w_br_b": w_br_b[0].astype(BF16),
        "w_o": w_o[0].astype(BF16),
        "w_up": w_up[0].astype(BF16),
        "w_down": w_down[0].astype(BF16),
    }
    y_prompt = _layer(x_prompt, mod[:nbp], p)
    y_sample = _layer(x_sample, mod[nbp:nbp + nbs], p)
    return (y_prompt, y_sample)
```

```python
import functools
import math

import jax
import jax.numpy as jnp
from jax import lax
from jax.experimental import pallas as pl
from jax.experimental.pallas import tpu as pltpu

F32 = jnp.float32
BF16 = jnp.bfloat16

D_MODEL = 1024
HEAD_DIM = 64
DA_HEADS = 8
DA_VDIM = 2 * HEAD_DIM
DIL_CONFIG = ((128, 1), (512, 4), (2048, 16))
N_DIL = 3
DIL_HEADS = 8
D_FF = 4 * D_MODEL
N_BUCKETS = 32
NEG_INF = -1e30
EPS = 1e-6
LAM_INIT = 0.8 - 0.6 * math.exp(-0.3 * 0)

DA_QK = DA_HEADS * HEAD_DIM
DA_V = DA_HEADS * DA_VDIM
DIL_W = DIL_HEADS * HEAD_DIM
DIL_COLS = 3 * N_DIL * DIL_W
IN_COLS = 4 * DA_QK + DA_V + DIL_COLS + 2 * D_MODEL

LANES = 128
SUBLANES = 8
COL_BLK = 512
N_COL_BLKS = IN_COLS // COL_BLK
NORM_W = 256

TM_PROJ = 512
TM_OUT = 256
T_ATT = 512
TQ_ATT = 1024
TQ_DIL = 128
ROWS_DIL = 1024
HALF_DIL = 64
W_DIL = TQ_DIL + 2 * HALF_DIL

LOG2E = math.log2(math.e)
LN2 = math.log(2.0)
DA_HEADROOM = 64.0
DA_MIN_EXP = -60.0
BF16_ROUND_MARGIN = 1.02

_MAX_EXACT = N_BUCKETS // 4
_BUCKET_STEPS = (12, 16, 23, 32, 46, 64, 91)


def _cparams(sem, vmem_mb):
    return pltpu.CompilerParams(dimension_semantics=sem, vmem_limit_bytes=vmem_mb * 1024 * 1024)


def _mod_kernel(c_ref, w_ref, b_ref, o_ref):
    c = c_ref[...]
    a = c * (1.0 / (1.0 + jnp.exp(-c)))
    o_ref[...] = jnp.dot(a, w_ref[...], preferred_element_type=F32,
                         precision=lax.Precision.HIGHEST) + b_ref[...]


def _modulation(c_all, w_ada, b_ada):
    rows = c_all.shape[0]
    n = w_ada.shape[1]
    return pl.pallas_call(
        _mod_kernel,
        grid=(n // D_MODEL,),
        in_specs=[pl.BlockSpec((rows, D_MODEL), lambda j: (0, 0)),
                  pl.BlockSpec((D_MODEL, D_MODEL), lambda j: (0, j)),
                  pl.BlockSpec((1, D_MODEL), lambda j: (0, j))],
        out_specs=pl.BlockSpec((rows, D_MODEL), lambda j: (0, j)),
        out_shape=jax.ShapeDtypeStruct((rows, n), F32),
        compiler_params=_cparams(("arbitrary",), 32),
        name="modulation",
    )(c_all, w_ada, b_ada.reshape(1, n))


def _rel_bucket(rel):
    n = jnp.abs(rel)
    large = jnp.full(rel.shape, _MAX_EXACT, jnp.int32)
    for t in _BUCKET_STEPS:
        large = large + (n >= t).astype(jnp.int32)
    return jnp.where(n < _MAX_EXACT, n, large) + jnp.where(rel > 0, N_BUCKETS // 2, 0)


def _table_values(bucket, tab_ref, col):
    val = jnp.zeros(bucket.shape, F32)
    for b in range(N_BUCKETS):
        val = jnp.where(bucket == b, tab_ref[b, col], val)
    return val


def _bias_lookup(rel, tab_ref, col):
    return _table_values(_rel_bucket(rel), tab_ref, col)


def _da_bias_kernel(tab_ref, dac_ref, o_ref):
    h = pl.program_id(0)
    sat = _BUCKET_STEPS[-1]
    row = lax.broadcasted_iota(jnp.int32, (LANES, LANES), 0)
    col = lax.broadcasted_iota(jnp.int32, (LANES, LANES), 1)
    far = (jnp.full((LANES, LANES), -dac_ref[h, 1], F32), jnp.full((LANES, LANES), -dac_ref[h, 2], F32))
    for d in range(-2, 3):
        for br in range(T_ATT // LANES):
            for bc in range(T_ATT // LANES):
                base = d * T_ATT + (bc - br) * LANES
                if base + LANES - 1 <= -sat:
                    val = far[0]
                elif base - (LANES - 1) >= sat:
                    val = far[1]
                else:
                    val = LOG2E * _bias_lookup(col - row + base, tab_ref, h) - dac_ref[h, 0]
                o_ref[0, d + 2, br * LANES:(br + 1) * LANES, bc * LANES:(bc + 1) * LANES] = val


def _da_bias_tiles(rel_bias, dac):
    assert T_ATT >= _BUCKET_STEPS[-1]
    return pl.pallas_call(
        _da_bias_kernel,
        grid=(DA_HEADS,),
        in_specs=[pl.BlockSpec(memory_space=pltpu.SMEM), pl.BlockSpec(memory_space=pltpu.SMEM)],
        out_specs=pl.BlockSpec((1, 5, T_ATT, T_ATT), lambda h: (h, 0, 0, 0)),
        out_shape=jax.ShapeDtypeStruct((DA_HEADS, 5, T_ATT, T_ATT), F32),
        compiler_params=_cparams(("arbitrary",), 32),
        name="da_bias_tiles",
    )(rel_bias, dac)


def _dil_bias_kernel(tab_ref, dilc_ref, o_ref):
    g = pl.program_id(0)
    v = pl.program_id(1)
    row = lax.broadcasted_iota(jnp.int32, (TQ_DIL, W_DIL), 0)
    col = lax.broadcasted_iota(jnp.int32, (TQ_DIL, W_DIL), 1)
    rel = col - row - HALF_DIL * v
    dilation = lax.shift_left(jnp.int32(1), 2 * g)
    bucket = _rel_bucket(rel * dilation)
    for h in range(DIL_HEADS):
        val = LOG2E * _table_values(bucket, tab_ref, DA_HEADS + g * DIL_HEADS + h) - dilc_ref[g, h, 0]
        o_ref[0, 0, h] = jnp.where(jnp.abs(rel) <= HALF_DIL, val, NEG_INF)


def _dil_bias_tiles(rel_bias, dilc):
    return pl.pallas_call(
        _dil_bias_kernel,
        grid=(N_DIL, 3),
        in_specs=[pl.BlockSpec(memory_space=pltpu.SMEM), pl.BlockSpec(memory_space=pltpu.SMEM)],
        out_specs=pl.BlockSpec((1, 1, DIL_HEADS, TQ_DIL, W_DIL), lambda g, v: (g, v, 0, 0, 0)),
        out_shape=jax.ShapeDtypeStruct((N_DIL, 3, DIL_HEADS, TQ_DIL, W_DIL), F32),
        compiler_params=_cparams(("arbitrary",) * 2, 32),
        name="dil_bias_tiles",
    )(rel_bias, dilc)


_DIL_BLK0 = (4 * DA_QK + DA_V) // COL_BLK
_GATE_BLK0 = _DIL_BLK0 + DIL_COLS // COL_BLK
_QK_BLOCKS = {0: 0, 1: 1, 2: 2, 3: 3}
for _g in range(N_DIL):
    _QK_BLOCKS[_DIL_BLK0 + 3 * _g] = 4 + 2 * _g
    _QK_BLOCKS[_DIL_BLK0 + 3 * _g + 1] = 5 + 2 * _g


def _inproj_kernel(x_ref, mod_ref, g1_ref, w_ref, gain_ref, gmat_ref,
                   qa_ref, ka_ref, va_ref, dil0_ref, dil1_ref, dil2_ref, gate_ref, cls_ref):
    tm = x_ref.shape[1]
    dil_refs = (dil0_ref, dil1_ref, dil2_ref)
    x = x_ref[0]
    mod = mod_ref[0]
    ms = jnp.mean(x * x, axis=-1, keepdims=True)
    h = (x * lax.rsqrt(ms + EPS)) * g1_ref[...]
    h = h * (1.0 + mod[1:2]) + mod[0:1]
    gmat = gmat_ref[...]

    for k in range(D_MODEL // LANES):
        cls_ref[k] = h[:, k * LANES:(k + 1) * LANES]

    def class_major(r):
        chunks = [jnp.concatenate([cls_ref[k, pl.ds(c, tm // r, stride=r), :] for c in range(r)], axis=0)
                  for k in range(D_MODEL // LANES)]
        return jnp.concatenate(chunks, axis=1).astype(BF16)

    hb_by_dilation = {1: h.astype(BF16)}
    for _, r in DIL_CONFIG:
        if r not in hb_by_dilation:
            hb_by_dilation[r] = class_major(r)

    order = sorted(range(N_COL_BLKS), key=lambda b: (0 if b >= _GATE_BLK0 else 1 if b in _QK_BLOCKS else 2, b))
    for blk in order:
        dilation = 1
        if blk < _DIL_BLK0:
            dst, off = (qa_ref, ka_ref, va_ref)[blk // 2], (blk % 2) * COL_BLK
        elif blk < _GATE_BLK0:
            g = (blk - _DIL_BLK0) // 3
            dst, off, dilation = dil_refs[g], ((blk - _DIL_BLK0) % 3) * COL_BLK, DIL_CONFIG[g][1]
        else:
            dst, off = gate_ref, (blk - _GATE_BLK0) * COL_BLK
        acc = jnp.dot(hb_by_dilation[dilation], w_ref[:, blk * COL_BLK:(blk + 1) * COL_BLK],
                      preferred_element_type=F32)

        def put(lo, y):
            cols = slice(off + lo, off + lo + y.shape[1])
            if dilation == 1:
                dst[0, :, cols] = y.astype(BF16)
            else:
                n = tm // dilation
                for c in range(dilation):
                    dst[0, c, :, cols] = y[c * n:(c + 1) * n].astype(BF16)

        if blk in _QK_BLOCKS:
            gi = _QK_BLOCKS[blk]
            for half in range(COL_BLK // NORM_W):
                lo = half * NORM_W
                a = acc[:, lo:lo + NORM_W]
                hm = jnp.dot((a * a).astype(BF16), gmat, preferred_element_type=F32)
                put(lo, (a * lax.rsqrt(hm + EPS)) * gain_ref[gi:gi + 1, lo:lo + NORM_W])
        elif blk >= _GATE_BLK0:
            put(0, 1.0 / (1.0 + jnp.exp(-acc)))
        else:
            put(0, acc)


def _inproj(x, mod, g1, w_in_b, gains, gmat):
    B, S, _ = x.shape
    tm = TM_PROJ
    const = lambda b, i: (0, 0)
    tok = lambda b, i: (b, i, 0)
    dil_shapes = tuple(jax.ShapeDtypeStruct((B, S, 3 * DIL_W) if r == 1 else (B, r, S // r, 3 * DIL_W), BF16)
                       for _, r in DIL_CONFIG)
    dil_specs = tuple(pl.BlockSpec((1, tm, 3 * DIL_W), tok) if r == 1 else
                      pl.BlockSpec((1, r, tm // r, 3 * DIL_W), lambda b, i: (b, 0, i, 0))
                      for _, r in DIL_CONFIG)
    tok_shapes = (jax.ShapeDtypeStruct((B, S, 2 * DA_QK), BF16),
                  jax.ShapeDtypeStruct((B, S, 2 * DA_QK), BF16),
                  jax.ShapeDtypeStruct((B, S, DA_V), BF16))
    gate_shape = jax.ShapeDtypeStruct((B, S, 2 * D_MODEL), BF16)
    tok_spec = lambda s: pl.BlockSpec((1, tm, s.shape[-1]), tok)
    return pl.pallas_call(
        _inproj_kernel,
        grid=(B, S // tm),
        in_specs=[pl.BlockSpec((1, tm, D_MODEL), tok),
                  pl.BlockSpec((1, 6, D_MODEL), lambda b, i: (b, 0, 0)),
                  pl.BlockSpec((1, D_MODEL), const),
                  pl.BlockSpec((D_MODEL, IN_COLS), const, pipeline_mode=pl.Buffered(1)),
                  pl.BlockSpec(gains.shape, const),
                  pl.BlockSpec((NORM_W, NORM_W), const)],
        out_specs=tuple(map(tok_spec, tok_shapes)) + dil_specs + (tok_spec(gate_shape),),
        out_shape=tok_shapes + dil_shapes + (gate_shape,),
        scratch_shapes=[pltpu.VMEM((D_MODEL // LANES, tm, LANES), F32)],
        compiler_params=_cparams(("arbitrary", "arbitrary"), 56),
        name="in_projection",
    )(x, mod, g1, w_in_b, gains, gmat)


def _diffattn_kernel(dac_ref, lam_ref, q_ref, k_ref, v_ref, bias_ref, g_ref, o_ref, vext_ref, acc_ref):
    T = T_ATT
    TQ = TQ_ATT
    n_sub = TQ // T
    h = pl.program_id(1)
    S = k_ref.shape[1]
    n_kv = S // T

    vext_ref[:, :DA_VDIM] = v_ref[0]
    vext_ref[:, DA_VDIM:] = jnp.ones((S, DA_VDIM), BF16)

    lp = lam_ref[...]
    lam = (jnp.exp(jnp.sum(lp[0:1] * lp[1:2], axis=-1, keepdims=True))
           - jnp.exp(jnp.sum(lp[2:3] * lp[3:4], axis=-1, keepdims=True)) + LAM_INIT)

    def kv_rows(j):
        return slice(j * T, (j + 1) * T) if isinstance(j, int) else pl.ds(pl.multiple_of(j * T, T), T)

    def q_tile(use_max, i, carry):
        q_rows = pl.ds(pl.multiple_of(i * TQ, TQ), TQ)
        q = q_ref[0, q_rows, :]
        lane = lax.broadcasted_iota(jnp.int32, q.shape, 1)
        zero = jnp.zeros_like(q)
        qs = jnp.concatenate([jnp.where(lane < HEAD_DIM, q, zero), jnp.where(lane >= HEAD_DIM, q, zero)], axis=0)

        def logits(j):
            s = lax.dot_general(qs, k_ref[0, kv_rows(j), :], (((1,), (1,)), ((), ())),
                                preferred_element_type=F32)
            tiles = [bias_ref[0, jnp.clip(j - (i * n_sub + t), -2, 2) + 2] for t in range(n_sub)]
            return s + jnp.concatenate(tiles + tiles, axis=0)

        if not use_max:
            for j in range(n_kv):
                pv = jnp.dot(jnp.exp2(logits(j)).astype(BF16), vext_ref[kv_rows(j), :],
                             preferred_element_type=F32)
                if j == 0:
                    acc_ref[...] = pv
                else:
                    acc_ref[...] += pv
        else:
            def body(j, mc):
                m, acc = mc
                u = logits(j)
                m_new = jnp.maximum(m, jnp.max(u, axis=-1, keepdims=True))
                p = jnp.exp2(u - m_new)
                acc = jnp.exp2(m - m_new) * acc + jnp.dot(p.astype(BF16), vext_ref[kv_rows(j), :],
                                                           preferred_element_type=F32)
                return m_new, acc

            init = (jnp.full((2 * TQ, 1), NEG_INF, F32), jnp.zeros(acc_ref.shape, F32))
            acc_ref[...] = lax.fori_loop(0, n_kv, body, init)[1]

        o = (acc_ref[:TQ, :DA_VDIM] / acc_ref[:TQ, DA_VDIM:]
             - lam * (acc_ref[TQ:, :DA_VDIM] / acc_ref[TQ:, DA_VDIM:]))
        ms = jnp.mean(o * o, axis=-1, keepdims=True)
        o = (o * lax.rsqrt(ms + EPS)) * g_ref[...] * (1.0 - LAM_INIT)
        o_ref[0, q_rows, :] = o.astype(o_ref.dtype)
        return carry

    def all_q_tiles(use_max):
        lax.fori_loop(0, S // TQ, functools.partial(q_tile, use_max), 0)

    fast_ok = dac_ref[h, 3] > 0.5
    pl.when(fast_ok)(functools.partial(all_q_tiles, False))
    pl.when(jnp.logical_not(fast_ok))(functools.partial(all_q_tiles, True))


def _diff_attention(dac, lam_params, qa, ka, va, bias_tiles, subln_g):
    B, S, _ = qa.shape
    T = T_ATT
    TQ = TQ_ATT
    return pl.pallas_call(
        _diffattn_kernel,
        grid=(B, DA_HEADS),
        in_specs=[pl.BlockSpec(memory_space=pltpu.SMEM),
                  pl.BlockSpec((4, HEAD_DIM), lambda b, h: (0, 0)),
                  pl.BlockSpec((1, S, LANES), lambda b, h: (b, 0, h)),
                  pl.BlockSpec((1, S, LANES), lambda b, h: (b, 0, h)),
                  pl.BlockSpec((1, S, LANES), lambda b, h: (b, 0, h)),
                  pl.BlockSpec((1, 5, T, T), lambda b, h: (h, 0, 0, 0)),
                  pl.BlockSpec((1, DA_VDIM), lambda b, h: (0, 0))],
        out_specs=pl.BlockSpec((1, S, LANES), lambda b, h: (b, 0, h)),
        out_shape=jax.ShapeDtypeStruct((B, S, DA_V), BF16),
        scratch_shapes=[pltpu.VMEM((S, 2 * DA_VDIM), BF16), pltpu.VMEM((2 * TQ, 2 * DA_VDIM), F32)],
        compiler_params=_cparams(("arbitrary",) * 2, 56),
        name="diff_attention",
    )(dac, lam_params, qa, ka, va, bias_tiles, subln_g)


def _dilattn_kernel(g, dilc_ref, q_ref, k_ref, v_ref, bias_ref, o_ref, lse_ref):
    TQ = TQ_DIL
    L = k_ref.shape[2]
    tiles_per_step = q_ref.shape[2] // TQ
    n_q = L // TQ
    low = lax.broadcasted_iota(jnp.int32, (TQ, LANES), 1) < HEAD_DIM
    ones = jnp.ones((W_DIL, LANES), BF16)

    def run(use_max):
        for cls in range(q_ref.shape[1]):
            for sub in range(tiles_per_step):
                for hp in range(DIL_HEADS // 2):
                    one_tile(use_max, cls, sub, hp)

    def one_tile(use_max, cls, sub, hp):
            i = pl.program_id(2) * tiles_per_step + sub
            w0 = pl.multiple_of(jnp.clip(i * TQ - HALF_DIL, 0, L - W_DIL), HALF_DIL)
            variant = jnp.where(i == 0, 0, jnp.where(i == n_q - 1, 2, 1))
            cs = slice(hp * LANES, (hp + 1) * LANES)
            qp = q_ref[0, cls, sub * TQ:(sub + 1) * TQ, cs]
            kp = k_ref[0, cls, pl.ds(w0, W_DIL), cs]
            vext = jnp.concatenate([v_ref[0, cls, pl.ds(w0, W_DIL), cs], ones], axis=1)
            zero = jnp.zeros_like(qp)
            qs = jnp.concatenate([jnp.where(low, qp, zero), jnp.where(low, zero, qp)], axis=0)
            u = lax.dot_general(qs, kp, (((1,), (1,)), ((), ())), preferred_element_type=F32)
            u = u + jnp.concatenate([bias_ref[0, variant, 2 * hp], bias_ref[0, variant, 2 * hp + 1]], axis=0)
            if use_max:
                m = jnp.max(u, axis=-1, keepdims=True)
                u = u - m
            r = jnp.dot(jnp.exp2(u).astype(BF16), vext, preferred_element_type=F32)
            num, den = r[:, :LANES], r[:, LANES:]
            lse2 = jnp.log2(den)
            if use_max:
                lse2 = lse2 + m
            o_pair = jnp.where(low, num[:TQ] / den[:TQ], num[TQ:] / den[TQ:]).astype(o_ref.dtype)
            lse_pair = LN2 * jnp.where(low, lse2[:TQ] + dilc_ref[g, 2 * hp, 0], lse2[TQ:] + dilc_ref[g, 2 * hp + 1, 0])
            if len(o_ref.shape) == 7:
                n_grp = o_ref.shape[3]
                for t in range(TQ // (n_grp * SUBLANES)):
                    for grp in range(n_grp):
                        lo = (t * n_grp + grp) * SUBLANES
                        o_ref[0, sub * (TQ // (n_grp * SUBLANES)) + t, hp, grp, cls] = o_pair[lo:lo + SUBLANES]
                        lse_ref[0, sub * (TQ // (n_grp * SUBLANES)) + t, hp, grp, cls] = lse_pair[lo:lo + SUBLANES]
            else:
                o_ref[0, cls, sub * TQ:(sub + 1) * TQ, cs] = o_pair
                lse_ref[0, cls, sub * TQ:(sub + 1) * TQ, cs] = lse_pair

    fast_ok = dilc_ref[g, 0, 1] > 0.5
    pl.when(fast_ok)(functools.partial(run, False))
    pl.when(jnp.logical_not(fast_ok))(functools.partial(run, True))


def _dilated_attention(dilc, dil, bias_tiles, g):
    B, r, L, _ = dil.shape
    kv_mode = pl.Buffered(1) if L * DIL_W * 2 > (4 << 20) else pl.Buffered(2)
    qmap = lambda b, c, i: (b, c, i, 0)
    rows = min(ROWS_DIL, L)
    cps = min(r, max(1, ROWS_DIL // L))
    if r == 1:
        o_spec = pl.BlockSpec((1, 1, rows, DIL_W), qmap)
        o_shape, o_dtype = (B, r, L, DIL_W), BF16
    else:
        n = TM_OUT // r
        nk = DIL_W // LANES
        o_spec = pl.BlockSpec((1, rows // n, nk, n // SUBLANES, cps, SUBLANES, LANES),
                              lambda b, c, i: (b, i, 0, 0, c, 0, 0))
        o_shape, o_dtype = (B, L // n, nk, n // SUBLANES, r, SUBLANES, LANES), F32
    return pl.pallas_call(
        functools.partial(_dilattn_kernel, g),
        grid=(B, r // cps, L // rows),
        in_specs=[pl.BlockSpec(memory_space=pltpu.SMEM),
                  pl.BlockSpec((1, cps, rows, DIL_W), qmap),
                  pl.BlockSpec((1, cps, L, DIL_W), lambda b, c, i: (b, c, 0, 1), pipeline_mode=kv_mode),
                  pl.BlockSpec((1, cps, L, DIL_W), lambda b, c, i: (b, c, 0, 2), pipeline_mode=kv_mode),
                  pl.BlockSpec((1, 3, DIL_HEADS, TQ_DIL, W_DIL), lambda b, c, i: (g, 0, 0, 0, 0))],
        out_specs=(o_spec, o_spec),
        out_shape=(jax.ShapeDtypeStruct(o_shape, o_dtype), jax.ShapeDtypeStruct(o_shape, F32)),
        compiler_params=_cparams(("arbitrary",) * 3, 48),
        name=f"dilated_attention_r{r}",
    )(dilc, dil, dil, dil, bias_tiles)


def _interleave_classes(ref, r, k, il_ref):
    nk = DIL_W // LANES
    n = ref.shape[2] // (r * nk)
    pieces = [ref[0, 0, pl.ds(k * n * r + (l // SUBLANES) * r * SUBLANES + l % SUBLANES, r, stride=SUBLANES), :]
              for l in range(n)]
    if r % 8 == 0:
        return jnp.concatenate(pieces, axis=0)
    for l in range(n):
        il_ref[pl.ds(l * r, r), :] = pieces[l]
    return il_ref[...]


def _merge_mlp_kernel(x_ref, mod_ref, ya_ref, o0_ref, l0_ref, o1_ref, l1_ref, o2_ref, l2_ref, gate_ref,
                      wa_ref, wb_ref, wo_ref, g2_ref, wu_ref, wd_ref, out_ref, il_ref, yb_ref):
    mod = mod_ref[0]

    @pl.when(pl.program_id(0) == 0)
    def _():
        yb_ref[...] = jnp.zeros(yb_ref.shape, BF16)

    yb = yb_ref[...]
    r1, r2 = DIL_CONFIG[1][1], DIL_CONFIG[2][1]

    def combine_chunk(k):
        cs = slice(k * LANES, (k + 1) * LANES)
        outs = (o0_ref[0, :, cs].astype(F32), _interleave_classes(o1_ref, r1, k, il_ref.at[0, k]),
                _interleave_classes(o2_ref, r2, k, il_ref.at[0, k]))
        lses = (l0_ref[0, :, cs], _interleave_classes(l1_ref, r1, k, il_ref.at[1, k]),
                _interleave_classes(l2_ref, r2, k, il_ref.at[1, k]))
        mx = jnp.maximum(jnp.maximum(lses[0], lses[1]), lses[2])
        es = [jnp.exp(l - mx) for l in lses]
        den = es[0] + es[1] + es[2]
        yb_ref[:, cs] = ((es[0] * outs[0] + es[1] * outs[1] + es[2] * outs[2]) / den).astype(BF16)

    pa = jnp.dot(ya_ref[0], wa_ref[...], preferred_element_type=F32)
    pb = jnp.dot(yb, wb_ref[...], preferred_element_type=F32)
    gates = gate_ref[0]
    merged = gates[:, :D_MODEL].astype(F32) * pa + gates[:, D_MODEL:].astype(F32) * pb
    x = x_ref[0] + mod[2:3] * jnp.dot(merged.astype(BF16), wo_ref[...], preferred_element_type=F32)

    ms = jnp.mean(x * x, axis=-1, keepdims=True)
    h = (x * lax.rsqrt(ms + EPS)) * g2_ref[...]
    hb = (h * (1.0 + mod[4:5]) + mod[3:4]).astype(BF16)
    acc = jnp.zeros(x.shape, F32)
    for f in range(D_FF // D_MODEL):
        cs = slice(f * D_MODEL, (f + 1) * D_MODEL)
        u = jnp.maximum(jnp.dot(hb, wu_ref[:, cs], preferred_element_type=F32), 0.0)
        combine_chunk(f)
        acc = acc + jnp.dot((u * u).astype(BF16), wd_ref[cs, :], preferred_element_type=F32)
    out_ref[0] = x + mod[5:6] * acc


def _merge_mlp(x, mod, ya, o0, l0, o1, l1, o2, l2, gates, wa, wb, wo, g2, wu, wd):
    B, S, _ = x.shape
    tm = TM_OUT
    n = S // tm
    const = lambda t: (0, 0)
    prev = lambda t: jnp.maximum(t - 1, 0)
    cur = lambda t: jnp.minimum(t, B * n - 1)
    tspec = lambda w: pl.BlockSpec((1, tm, w), lambda t: (prev(t) // n, prev(t) % n, 0))
    dspec = lambda w: pl.BlockSpec((1, tm, w), lambda t: (cur(t) // n, cur(t) % n, 0))
    cspec = lambda a: pl.BlockSpec((1, 1) + a.shape[2:], lambda t: (cur(t) // n, cur(t) % n, 0, 0))
    wspec = lambda a: pl.BlockSpec(a.shape, const, pipeline_mode=pl.Buffered(1))
    return pl.pallas_call(
        _merge_mlp_kernel,
        grid=(B * n + 1,),
        in_specs=[tspec(D_MODEL), pl.BlockSpec((1, 6, D_MODEL), lambda t: (prev(t) // n, 0, 0)), tspec(DA_V),
                  dspec(DIL_W), dspec(DIL_W), cspec(o1), cspec(l1), cspec(o2), cspec(l2),
                  tspec(2 * D_MODEL), wspec(wa), wspec(wb), wspec(wo),
                  pl.BlockSpec((1, D_MODEL), const), wspec(wu), wspec(wd)],
        out_specs=tspec(D_MODEL),
        out_shape=jax.ShapeDtypeStruct((B, S, D_MODEL), F32),
        scratch_shapes=[pltpu.VMEM((2, DIL_W // LANES, tm, LANES), F32), pltpu.VMEM((tm, DIL_W), BF16)],
        compiler_params=_cparams(("arbitrary",), 56),
        name="merge_mlp",
    )(x, mod, ya, o0, l0, o1, l1, o2, l2, gates, wa, wb, wo, g2, wu, wd)


def _pair_heads(a, b):
    d = a.shape[0]
    return jnp.stack([a.reshape(d, DA_HEADS, HEAD_DIM), b.reshape(d, DA_HEADS, HEAD_DIM)],
                     axis=2).reshape(d, 2 * DA_QK)


def _layer(x, mod, p):
    B, S, _ = x.shape
    qa, ka, va, dil0, dil1, dil2, gates = _inproj(x, mod, p["g1"], p["w_in"], p["gains"], p["gmat"])
    ya = _diff_attention(p["dac"], p["lam"], qa, ka, va, p["da_bias"], p["subln_g"])
    o0, l0 = _dilated_attention(p["dilc"], dil0.reshape(B, 1, S, 3 * DIL_W), p["dil_bias"], 0)
    o1, l1 = _dilated_attention(p["dilc"], dil1, p["dil_bias"], 1)
    o2, l2 = _dilated_attention(p["dilc"], dil2, p["dil_bias"], 2)
    rows = lambda a: a.reshape(B, a.shape[1], -1, LANES)
    return _merge_mlp(x, mod, ya, o0.reshape(B, S, DIL_W), l0.reshape(B, S, DIL_W),
                      rows(o1), rows(l1), rows(o2), rows(l2), gates,
                      p["w_br_a"], p["w_br_b"], p["w_o"], p["g2"], p["w_up"], p["w_down"])


def kernel(x_prompt, x_sample, c_prompt, c_sample, rel_bias, norm1_g, w_ada, b_ada, w_in, qn_a, kn_a,
           lambda_q1, lambda_k1, lambda_q2, lambda_k2, subln_g, qn_b, kn_b, w_br_a, w_br_b, w_o,
           norm2_g, w_up, w_down):
    nbp, nbs = c_prompt.shape[0], c_sample.shape[0]
    pad = (-(nbp + nbs)) % 8
    c_all = jnp.concatenate([c_prompt, c_sample, jnp.zeros((pad, D_MODEL), F32)], axis=0)
    mod = _modulation(c_all, w_ada[0], b_ada[0]).reshape(-1, 6, D_MODEL)

    w = w_in[0]
    q1, q2, k1, k2 = (w[:, n * DA_QK:(n + 1) * DA_QK] for n in range(4))
    w_perm = jnp.concatenate([_pair_heads(q1, q2), _pair_heads(k1, k2), w[:, 4 * DA_QK:]], axis=1)

    scale = HEAD_DIM ** -0.5
    tile8 = lambda v: jnp.tile(v, DIL_W // HEAD_DIM)
    gain_rows = [tile8(qn_a[0]) * (scale * LOG2E)] * 2 + [tile8(kn_a[0])] * 2
    for g in range(N_DIL):
        gain_rows += [tile8(qn_b[0, g]) * (scale * LOG2E), tile8(kn_b[0, g])]
    ids = jnp.arange(NORM_W) // HEAD_DIM
    gmat = jnp.where(ids[:, None] == ids[None, :], 1.0 / HEAD_DIM, 0.0).astype(BF16)

    def logit_reference(q_gain, k_gain, tab):
        qk_bound = HEAD_DIM * scale * jnp.max(jnp.abs(q_gain)) * jnp.max(jnp.abs(k_gain)) * BF16_ROUND_MARGIN
        c_ref = LOG2E * (qk_bound + jnp.max(tab, axis=0)) - DA_HEADROOM
        u_min = LOG2E * (jnp.min(tab, axis=0) - qk_bound) - c_ref
        return c_ref, jnp.all(u_min >= DA_MIN_EXP).astype(F32) * jnp.ones_like(c_ref)

    da_tab = rel_bias[:, :DA_HEADS]
    c_ref, da_ok = logit_reference(qn_a[0], kn_a[0], da_tab)
    dac = jnp.stack([c_ref, c_ref - LOG2E * da_tab[N_BUCKETS // 2 - 1], c_ref - LOG2E * da_tab[N_BUCKETS - 1],
                     da_ok], axis=1)
    dilc = jnp.stack([jnp.stack(logit_reference(
        qn_b[0, g], kn_b[0, g], rel_bias[:, DA_HEADS + g * DIL_HEADS:DA_HEADS + (g + 1) * DIL_HEADS]), axis=1)
        for g in range(N_DIL)], axis=0)

    p = {
        "g1": norm1_g[0].reshape(1, D_MODEL),
        "g2": norm2_g[0].reshape(1, D_MODEL),
        "w_in": w_perm.astype(BF16),
        "gains": jnp.stack(gain_rows, axis=0),
        "gmat": gmat,
        "lam": jnp.stack([lambda_q1[0], lambda_k1[0], lambda_q2[0], lambda_k2[0]], axis=0),
        "subln_g": subln_g[0].reshape(1, DA_VDIM),
        "dac": dac,
        "da_bias": _da_bias_tiles(rel_bias, dac),
        "dilc": dilc,
        "dil_bias": _dil_bias_tiles(rel_bias, dilc),
        "w_br_a": w_br_a[0].astype(BF16),
        "w_br_b": w_br_b[0].astype(BF16),
        "w_o": w_o[0].astype(BF16),
        "w_up": w_up[0].astype(BF16),
        "w_down": w_down[0].astype(BF16),
    }
    y_prompt = _layer(x_prompt, mod[:nbp], p)
    y_sample = _layer(x_sample, mod[nbp:nbp + nbs], p)
    return (y_prompt, y_sample)
```

```python
import functools
import math

import jax
import jax.numpy as jnp
from jax import lax
from jax.experimental import pallas as pl
from jax.experimental.pallas import tpu as pltpu

F32 = jnp.float32
BF16 = jnp.bfloat16

D_MODEL = 1024
HEAD_DIM = 64
DA_HEADS = 8
DA_VDIM = 2 * HEAD_DIM
DIL_CONFIG = ((128, 1), (512, 4), (2048, 16))
N_DIL = 3
DIL_HEADS = 8
D_FF = 4 * D_MODEL
N_BUCKETS = 32
NEG_INF = -1e30
EPS = 1e-6
LAM_INIT = 0.8 - 0.6 * math.exp(-0.3 * 0)

DA_QK = DA_HEADS * HEAD_DIM
DA_V = DA_HEADS * DA_VDIM
DIL_W = DIL_HEADS * HEAD_DIM
DIL_COLS = 3 * N_DIL * DIL_W
IN_COLS = 4 * DA_QK + DA_V + DIL_COLS + 2 * D_MODEL

LANES = 128
SUBLANES = 8
COL_BLK = 512
N_COL_BLKS = IN_COLS // COL_BLK
NORM_W = 256

TM_PROJ = 512
TM_OUT = 256
T_ATT = 512
TQ_ATT = 2048
TQ_DIL = 128
ROWS_DIL = 1024
HALF_DIL = 64
W_DIL = TQ_DIL + 2 * HALF_DIL

LOG2E = math.log2(math.e)
LN2 = math.log(2.0)
DA_HEADROOM = 64.0
DA_MIN_EXP = -60.0
BF16_ROUND_MARGIN = 1.02

_MAX_EXACT = N_BUCKETS // 4
_BUCKET_STEPS = (12, 16, 23, 32, 46, 64, 91)


def _cparams(sem, vmem_mb):
    return pltpu.CompilerParams(dimension_semantics=sem, vmem_limit_bytes=vmem_mb * 1024 * 1024)


def _mod_kernel(c_ref, w_ref, b_ref, o_ref):
    c = c_ref[...]
    a = c * (1.0 / (1.0 + jnp.exp(-c)))
    o_ref[...] = jnp.dot(a, w_ref[...], preferred_element_type=F32,
                         precision=lax.Precision.HIGHEST) + b_ref[...]


def _modulation(c_all, w_ada, b_ada):
    rows = c_all.shape[0]
    n = w_ada.shape[1]
    return pl.pallas_call(
        _mod_kernel,
        grid=(n // D_MODEL,),
        in_specs=[pl.BlockSpec((rows, D_MODEL), lambda j: (0, 0)),
                  pl.BlockSpec((D_MODEL, D_MODEL), lambda j: (0, j)),
                  pl.BlockSpec((1, D_MODEL), lambda j: (0, j))],
        out_specs=pl.BlockSpec((rows, D_MODEL), lambda j: (0, j)),
        out_shape=jax.ShapeDtypeStruct((rows, n), F32),
        compiler_params=_cparams(("arbitrary",), 32),
        name="modulation",
    )(c_all, w_ada, b_ada.reshape(1, n))


def _rel_bucket(rel):
    n = jnp.abs(rel)
    large = jnp.full(rel.shape, _MAX_EXACT, jnp.int32)
    for t in _BUCKET_STEPS:
        large = large + (n >= t).astype(jnp.int32)
    return jnp.where(n < _MAX_EXACT, n, large) + jnp.where(rel > 0, N_BUCKETS // 2, 0)


def _table_values(bucket, tab_ref, col):
    val = jnp.zeros(bucket.shape, F32)
    for b in range(N_BUCKETS):
        val = jnp.where(bucket == b, tab_ref[b, col], val)
    return val


def _bias_lookup(rel, tab_ref, col):
    return _table_values(_rel_bucket(rel), tab_ref, col)


def _da_bias_kernel(tab_ref, dac_ref, o_ref):
    h = pl.program_id(0)
    sat = _BUCKET_STEPS[-1]
    row = lax.broadcasted_iota(jnp.int32, (LANES, LANES), 0)
    col = lax.broadcasted_iota(jnp.int32, (LANES, LANES), 1)
    far = (jnp.full((LANES, LANES), -dac_ref[h, 1], F32), jnp.full((LANES, LANES), -dac_ref[h, 2], F32))
    for d in range(-2, 3):
        for br in range(T_ATT // LANES):
            for bc in range(T_ATT // LANES):
                base = d * T_ATT + (bc - br) * LANES
                if base + LANES - 1 <= -sat:
                    val = far[0]
                elif base - (LANES - 1) >= sat:
                    val = far[1]
                else:
                    val = LOG2E * _bias_lookup(col - row + base, tab_ref, h) - dac_ref[h, 0]
                o_ref[0, d + 2, br * LANES:(br + 1) * LANES, bc * LANES:(bc + 1) * LANES] = val


def _da_bias_tiles(rel_bias, dac):
    assert T_ATT >= _BUCKET_STEPS[-1]
    return pl.pallas_call(
        _da_bias_kernel,
        grid=(DA_HEADS,),
        in_specs=[pl.BlockSpec(memory_space=pltpu.SMEM), pl.BlockSpec(memory_space=pltpu.SMEM)],
        out_specs=pl.BlockSpec((1, 5, T_ATT, T_ATT), lambda h: (h, 0, 0, 0)),
        out_shape=jax.ShapeDtypeStruct((DA_HEADS, 5, T_ATT, T_ATT), F32),
        compiler_params=_cparams(("arbitrary",), 32),
        name="da_bias_tiles",
    )(rel_bias, dac)


def _dil_bias_kernel(tab_ref, dilc_ref, o_ref):
    g = pl.program_id(0)
    v = pl.program_id(1)
    row = lax.broadcasted_iota(jnp.int32, (TQ_DIL, W_DIL), 0)
    col = lax.broadcasted_iota(jnp.int32, (TQ_DIL, W_DIL), 1)
    rel = col - row - HALF_DIL * v
    dilation = lax.shift_left(jnp.int32(1), 2 * g)
    bucket = _rel_bucket(rel * dilation)
    for h in range(DIL_HEADS):
        val = LOG2E * _table_values(bucket, tab_ref, DA_HEADS + g * DIL_HEADS + h) - dilc_ref[g, h, 0]
        o_ref[0, 0, h] = jnp.where(jnp.abs(rel) <= HALF_DIL, val, NEG_INF)


def _dil_bias_tiles(rel_bias, dilc):
    return pl.pallas_call(
        _dil_bias_kernel,
        grid=(N_DIL, 3),
        in_specs=[pl.BlockSpec(memory_space=pltpu.SMEM), pl.BlockSpec(memory_space=pltpu.SMEM)],
        out_specs=pl.BlockSpec((1, 1, DIL_HEADS, TQ_DIL, W_DIL), lambda g, v: (g, v, 0, 0, 0)),
        out_shape=jax.ShapeDtypeStruct((N_DIL, 3, DIL_HEADS, TQ_DIL, W_DIL), F32),
        compiler_params=_cparams(("arbitrary",) * 2, 32),
        name="dil_bias_tiles",
    )(rel_bias, dilc)


_DIL_BLK0 = (4 * DA_QK + DA_V) // COL_BLK
_GATE_BLK0 = _DIL_BLK0 + DIL_COLS // COL_BLK
_QK_BLOCKS = {0: 0, 1: 1, 2: 2, 3: 3}
for _g in range(N_DIL):
    _QK_BLOCKS[_DIL_BLK0 + 3 * _g] = 4 + 2 * _g
    _QK_BLOCKS[_DIL_BLK0 + 3 * _g + 1] = 5 + 2 * _g


def _inproj_kernel(x_ref, mod_ref, g1_ref, w_ref, gain_ref, gmat_ref,
                   qa_ref, ka_ref, va_ref, dil0_ref, dil1_ref, dil2_ref, gate_ref, cls_ref):
    tm = x_ref.shape[1]
    dil_refs = (dil0_ref, dil1_ref, dil2_ref)
    x = x_ref[0]
    mod = mod_ref[0]
    ms = jnp.mean(x * x, axis=-1, keepdims=True)
    h = (x * lax.rsqrt(ms + EPS)) * g1_ref[...]
    h = h * (1.0 + mod[1:2]) + mod[0:1]
    gmat = gmat_ref[...]

    for k in range(D_MODEL // LANES):
        cls_ref[k] = h[:, k * LANES:(k + 1) * LANES]

    def class_major(r):
        chunks = [jnp.concatenate([cls_ref[k, pl.ds(c, tm // r, stride=r), :] for c in range(r)], axis=0)
                  for k in range(D_MODEL // LANES)]
        return jnp.concatenate(chunks, axis=1).astype(BF16)

    hb_by_dilation = {1: h.astype(BF16)}
    for _, r in DIL_CONFIG:
        if r not in hb_by_dilation:
            hb_by_dilation[r] = class_major(r)

    order = sorted(range(N_COL_BLKS), key=lambda b: (0 if b >= _GATE_BLK0 else 1 if b in _QK_BLOCKS else 2, b))
    for blk in order:
        dilation = 1
        if blk < _DIL_BLK0:
            dst, off = (qa_ref, ka_ref, va_ref)[blk // 2], (blk % 2) * COL_BLK
        elif blk < _GATE_BLK0:
            g = (blk - _DIL_BLK0) // 3
            dst, off, dilation = dil_refs[g], ((blk - _DIL_BLK0) % 3) * COL_BLK, DIL_CONFIG[g][1]
        else:
            dst, off = gate_ref, (blk - _GATE_BLK0) * COL_BLK
        acc = jnp.dot(hb_by_dilation[dilation], w_ref[:, blk * COL_BLK:(blk + 1) * COL_BLK],
                      preferred_element_type=F32)

        def put(lo, y):
            cols = slice(off + lo, off + lo + y.shape[1])
            if dilation == 1:
                dst[0, :, cols] = y.astype(BF16)
            else:
                n = tm // dilation
                for c in range(dilation):
                    dst[0, c, :, cols] = y[c * n:(c + 1) * n].astype(BF16)

        if blk in _QK_BLOCKS:
            gi = _QK_BLOCKS[blk]
            for half in range(COL_BLK // NORM_W):
                lo = half * NORM_W
                a = acc[:, lo:lo + NORM_W]
                hm = jnp.dot((a * a).astype(BF16), gmat, preferred_element_type=F32)
                put(lo, (a * lax.rsqrt(hm + EPS)) * gain_ref[gi:gi + 1, lo:lo + NORM_W])
        elif blk >= _GATE_BLK0:
            put(0, 1.0 / (1.0 + jnp.exp(-acc)))
        else:
            put(0, acc)


def _inproj(x, mod, g1, w_in_b, gains, gmat):
    B, S, _ = x.shape
    tm = TM_PROJ
    const = lambda b, i: (0, 0)
    tok = lambda b, i: (b, i, 0)
    dil_shapes = tuple(jax.ShapeDtypeStruct((B, S, 3 * DIL_W) if r == 1 else (B, r, S // r, 3 * DIL_W), BF16)
                       for _, r in DIL_CONFIG)
    dil_specs = tuple(pl.BlockSpec((1, tm, 3 * DIL_W), tok) if r == 1 else
                      pl.BlockSpec((1, r, tm // r, 3 * DIL_W), lambda b, i: (b, 0, i, 0))
                      for _, r in DIL_CONFIG)
    tok_shapes = (jax.ShapeDtypeStruct((B, S, 2 * DA_QK), BF16),
                  jax.ShapeDtypeStruct((B, S, 2 * DA_QK), BF16),
                  jax.ShapeDtypeStruct((B, S, DA_V), BF16))
    gate_shape = jax.ShapeDtypeStruct((B, S, 2 * D_MODEL), BF16)
    tok_spec = lambda s: pl.BlockSpec((1, tm, s.shape[-1]), tok)
    return pl.pallas_call(
        _inproj_kernel,
        grid=(B, S // tm),
        in_specs=[pl.BlockSpec((1, tm, D_MODEL), tok),
                  pl.BlockSpec((1, 6, D_MODEL), lambda b, i: (b, 0, 0)),
                  pl.BlockSpec((1, D_MODEL), const),
                  pl.BlockSpec((D_MODEL, IN_COLS), const, pipeline_mode=pl.Buffered(1)),
                  pl.BlockSpec(gains.shape, const),
                  pl.BlockSpec((NORM_W, NORM_W), const)],
        out_specs=tuple(map(tok_spec, tok_shapes)) + dil_specs + (tok_spec(gate_shape),),
        out_shape=tok_shapes + dil_shapes + (gate_shape,),
        scratch_shapes=[pltpu.VMEM((D_MODEL // LANES, tm, LANES), F32)],
        compiler_params=_cparams(("arbitrary", "arbitrary"), 56),
        name="in_projection",
    )(x, mod, g1, w_in_b, gains, gmat)


def _diffattn_kernel(dac_ref, lam_ref, q_ref, k_ref, v_ref, bias_ref, g_ref, o_ref, vext_ref, acc_ref):
    T = T_ATT
    TQ = TQ_ATT
    n_sub = TQ // T
    h = pl.program_id(1)
    S = k_ref.shape[1]
    n_kv = S // T

    vext_ref[:, :DA_VDIM] = v_ref[0]
    vext_ref[:, DA_VDIM:] = jnp.ones((S, DA_VDIM), BF16)

    lp = lam_ref[...]
    lam = (jnp.exp(jnp.sum(lp[0:1] * lp[1:2], axis=-1, keepdims=True))
           - jnp.exp(jnp.sum(lp[2:3] * lp[3:4], axis=-1, keepdims=True)) + LAM_INIT)

    def kv_rows(j):
        return slice(j * T, (j + 1) * T) if isinstance(j, int) else pl.ds(pl.multiple_of(j * T, T), T)

    def q_tile(use_max, i, carry):
        q_rows = pl.ds(pl.multiple_of(i * TQ, TQ), TQ)
        q = q_ref[0, q_rows, :]
        lane = lax.broadcasted_iota(jnp.int32, q.shape, 1)
        zero = jnp.zeros_like(q)
        qs = jnp.concatenate([jnp.where(lane < HEAD_DIM, q, zero), jnp.where(lane >= HEAD_DIM, q, zero)], axis=0)

        def logits(j):
            s = lax.dot_general(qs, k_ref[0, kv_rows(j), :], (((1,), (1,)), ((), ())),
                                preferred_element_type=F32)
            tiles = [bias_ref[0, jnp.clip(j - (i * n_sub + t), -2, 2) + 2] for t in range(n_sub)]
            return s + jnp.concatenate(tiles + tiles, axis=0)

        if not use_max:
            for j in range(n_kv):
                pv = jnp.dot(jnp.exp2(logits(j)).astype(BF16), vext_ref[kv_rows(j), :],
                             preferred_element_type=F32)
                if j == 0:
                    acc_ref[...] = pv
                else:
                    acc_ref[...] += pv
        else:
            def body(j, mc):
                m, acc = mc
                u = logits(j)
                m_new = jnp.maximum(m, jnp.max(u, axis=-1, keepdims=True))
                p = jnp.exp2(u - m_new)
                acc = jnp.exp2(m - m_new) * acc + jnp.dot(p.astype(BF16), vext_ref[kv_rows(j), :],
                                                           preferred_element_type=F32)
                return m_new, acc

            init = (jnp.full((2 * TQ, 1), NEG_INF, F32), jnp.zeros(acc_ref.shape, F32))
            acc_ref[...] = lax.fori_loop(0, n_kv, body, init)[1]

        o = (acc_ref[:TQ, :DA_VDIM] / acc_ref[:TQ, DA_VDIM:]
             - lam * (acc_ref[TQ:, :DA_VDIM] / acc_ref[TQ:, DA_VDIM:]))
        ms = jnp.mean(o * o, axis=-1, keepdims=True)
        o = (o * lax.rsqrt(ms + EPS)) * g_ref[...] * (1.0 - LAM_INIT)
        o_ref[0, q_rows, :] = o.astype(o_ref.dtype)
        return carry

    def all_q_tiles(use_max):
        lax.fori_loop(0, S // TQ, functools.partial(q_tile, use_max), 0)

    fast_ok = dac_ref[h, 3] > 0.5
    pl.when(fast_ok)(functools.partial(all_q_tiles, False))
    pl.when(jnp.logical_not(fast_ok))(functools.partial(all_q_tiles, True))


def _diff_attention(dac, lam_params, qa, ka, va, bias_tiles, subln_g):
    B, S, _ = qa.shape
    T = T_ATT
    TQ = TQ_ATT
    return pl.pallas_call(
        _diffattn_kernel,
        grid=(B, DA_HEADS),
        in_specs=[pl.BlockSpec(memory_space=pltpu.SMEM),
                  pl.BlockSpec((4, HEAD_DIM), lambda b, h: (0, 0)),
                  pl.BlockSpec((1, S, LANES), lambda b, h: (b, 0, h)),
                  pl.BlockSpec((1, S, LANES), lambda b, h: (b, 0, h)),
                  pl.BlockSpec((1, S, LANES), lambda b, h: (b, 0, h)),
                  pl.BlockSpec((1, 5, T, T), lambda b, h: (h, 0, 0, 0)),
                  pl.BlockSpec((1, DA_VDIM), lambda b, h: (0, 0))],
        out_specs=pl.BlockSpec((1, S, LANES), lambda b, h: (b, 0, h)),
        out_shape=jax.ShapeDtypeStruct((B, S, DA_V), BF16),
        scratch_shapes=[pltpu.VMEM((S, 2 * DA_VDIM), BF16), pltpu.VMEM((2 * TQ, 2 * DA_VDIM), F32)],
        compiler_params=_cparams(("arbitrary",) * 2, 56),
        name="diff_attention",
    )(dac, lam_params, qa, ka, va, bias_tiles, subln_g)


def _dilattn_kernel(g, dilc_ref, q_ref, k_ref, v_ref, bias_ref, o_ref, lse_ref):
    TQ = TQ_DIL
    L = k_ref.shape[2]
    tiles_per_step = q_ref.shape[2] // TQ
    n_q = L // TQ
    low = lax.broadcasted_iota(jnp.int32, (TQ, LANES), 1) < HEAD_DIM
    ones = jnp.ones((W_DIL, LANES), BF16)

    def run(use_max):
        for cls in range(q_ref.shape[1]):
            for sub in range(tiles_per_step):
                for hp in range(DIL_HEADS // 2):
                    one_tile(use_max, cls, sub, hp)

    def one_tile(use_max, cls, sub, hp):
            i = pl.program_id(2) * tiles_per_step + sub
            w0 = pl.multiple_of(jnp.clip(i * TQ - HALF_DIL, 0, L - W_DIL), HALF_DIL)
            variant = jnp.where(i == 0, 0, jnp.where(i == n_q - 1, 2, 1))
            cs = slice(hp * LANES, (hp + 1) * LANES)
            qp = q_ref[0, cls, sub * TQ:(sub + 1) * TQ, cs]
            kp = k_ref[0, cls, pl.ds(w0, W_DIL), cs]
            vext = jnp.concatenate([v_ref[0, cls, pl.ds(w0, W_DIL), cs], ones], axis=1)
            zero = jnp.zeros_like(qp)
            qs = jnp.concatenate([jnp.where(low, qp, zero), jnp.where(low, zero, qp)], axis=0)
            u = lax.dot_general(qs, kp, (((1,), (1,)), ((), ())), preferred_element_type=F32)
            u = u + jnp.concatenate([bias_ref[0, variant, 2 * hp], bias_ref[0, variant, 2 * hp + 1]], axis=0)
            if use_max:
                m = jnp.max(u, axis=-1, keepdims=True)
                u = u - m
            r = jnp.dot(jnp.exp2(u).astype(BF16), vext, preferred_element_type=F32)
            num, den = r[:, :LANES], r[:, LANES:]
            lse2 = jnp.log2(den)
            if use_max:
                lse2 = lse2 + m
            o_pair = jnp.where(low, num[:TQ] / den[:TQ], num[TQ:] / den[TQ:]).astype(o_ref.dtype)
            lse_pair = LN2 * jnp.where(low, lse2[:TQ] + dilc_ref[g, 2 * hp, 0], lse2[TQ:] + dilc_ref[g, 2 * hp + 1, 0])
            if len(o_ref.shape) == 7:
                n_grp = o_ref.shape[3]
                for t in range(TQ // (n_grp * SUBLANES)):
                    for grp in range(n_grp):
                        lo = (t * n_grp + grp) * SUBLANES
                        o_ref[0, sub * (TQ // (n_grp * SUBLANES)) + t, hp, grp, cls] = o_pair[lo:lo + SUBLANES]
                        lse_ref[0, sub * (TQ // (n_grp * SUBLANES)) + t, hp, grp, cls] = lse_pair[lo:lo + SUBLANES]
            else:
                o_ref[0, cls, sub * TQ:(sub + 1) * TQ, cs] = o_pair
                lse_ref[0, cls, sub * TQ:(sub + 1) * TQ, cs] = lse_pair

    fast_ok = dilc_ref[g, 0, 1] > 0.5
    pl.when(fast_ok)(functools.partial(run, False))
    pl.when(jnp.logical_not(fast_ok))(functools.partial(run, True))


def _dilated_attention(dilc, dil, bias_tiles, g):
    B, r, L, _ = dil.shape
    kv_mode = pl.Buffered(1) if L * DIL_W * 2 > (4 << 20) else pl.Buffered(2)
    qmap = lambda b, c, i: (b, c, i, 0)
    rows = min(ROWS_DIL, L)
    cps = min(r, max(1, ROWS_DIL // L))
    if r == 1:
        o_spec = pl.BlockSpec((1, 1, rows, DIL_W), qmap)
        o_shape, o_dtype = (B, r, L, DIL_W), BF16
    else:
        n = TM_OUT // r
        nk = DIL_W // LANES
        o_spec = pl.BlockSpec((1, rows // n, nk, n // SUBLANES, cps, SUBLANES, LANES),
                              lambda b, c, i: (b, i, 0, 0, c, 0, 0))
        o_shape, o_dtype = (B, L // n, nk, n // SUBLANES, r, SUBLANES, LANES), F32
    return pl.pallas_call(
        functools.partial(_dilattn_kernel, g),
        grid=(B, r // cps, L // rows),
        in_specs=[pl.BlockSpec(memory_space=pltpu.SMEM),
                  pl.BlockSpec((1, cps, rows, DIL_W), qmap),
                  pl.BlockSpec((1, cps, L, DIL_W), lambda b, c, i: (b, c, 0, 1), pipeline_mode=kv_mode),
                  pl.BlockSpec((1, cps, L, DIL_W), lambda b, c, i: (b, c, 0, 2), pipeline_mode=kv_mode),
                  pl.BlockSpec((1, 3, DIL_HEADS, TQ_DIL, W_DIL), lambda b, c, i: (g, 0, 0, 0, 0))],
        out_specs=(o_spec, o_spec),
        out_shape=(jax.ShapeDtypeStruct(o_shape, o_dtype), jax.ShapeDtypeStruct(o_shape, F32)),
        compiler_params=_cparams(("arbitrary",) * 3, 48),
        name=f"dilated_attention_r{r}",
    )(dilc, dil, dil, dil, bias_tiles)


def _interleave_classes(ref, r, k, il_ref):
    nk = DIL_W // LANES
    n = ref.shape[2] // (r * nk)
    pieces = [ref[0, 0, pl.ds(k * n * r + (l // SUBLANES) * r * SUBLANES + l % SUBLANES, r, stride=SUBLANES), :]
              for l in range(n)]
    if r % 8 == 0:
        return jnp.concatenate(pieces, axis=0)
    for l in range(n):
        il_ref[pl.ds(l * r, r), :] = pieces[l]
    return il_ref[...]


def _merge_mlp_kernel(x_ref, mod_ref, ya_ref, o0_ref, l0_ref, o1_ref, l1_ref, o2_ref, l2_ref, gate_ref,
                      wa_ref, wb_ref, wo_ref, g2_ref, wu_ref, wd_ref, out_ref, il_ref, yb_ref):
    mod = mod_ref[0]

    @pl.when(pl.program_id(0) == 0)
    def _():
        yb_ref[...] = jnp.zeros(yb_ref.shape, BF16)

    yb = yb_ref[...]
    r1, r2 = DIL_CONFIG[1][1], DIL_CONFIG[2][1]

    def combine_chunk(k):
        cs = slice(k * LANES, (k + 1) * LANES)
        outs = (o0_ref[0, :, cs].astype(F32), _interleave_classes(o1_ref, r1, k, il_ref.at[0, k]),
                _interleave_classes(o2_ref, r2, k, il_ref.at[0, k]))
        lses = (l0_ref[0, :, cs], _interleave_classes(l1_ref, r1, k, il_ref.at[1, k]),
                _interleave_classes(l2_ref, r2, k, il_ref.at[1, k]))
        mx = jnp.maximum(jnp.maximum(lses[0], lses[1]), lses[2])
        es = [jnp.exp(l - mx) for l in lses]
        den = es[0] + es[1] + es[2]
        yb_ref[:, cs] = ((es[0] * outs[0] + es[1] * outs[1] + es[2] * outs[2]) / den).astype(BF16)

    pa = jnp.dot(ya_ref[0], wa_ref[...], preferred_element_type=F32)
    pb = jnp.dot(yb, wb_ref[...], preferred_element_type=F32)
    gates = gate_ref[0]
    merged = gates[:, :D_MODEL].astype(F32) * pa + gates[:, D_MODEL:].astype(F32) * pb
    x = x_ref[0] + mod[2:3] * jnp.dot(merged.astype(BF16), wo_ref[...], preferred_element_type=F32)

    ms = jnp.mean(x * x, axis=-1, keepdims=True)
    h = (x * lax.rsqrt(ms + EPS)) * g2_ref[...]
    hb = (h * (1.0 + mod[4:5]) + mod[3:4]).astype(BF16)
    acc = jnp.zeros(x.shape, F32)
    for f in range(D_FF // D_MODEL):
        cs = slice(f * D_MODEL, (f + 1) * D_MODEL)
        u = jnp.maximum(jnp.dot(hb, wu_ref[:, cs], preferred_element_type=F32), 0.0)
        combine_chunk(f)
        acc = acc + jnp.dot((u * u).astype(BF16), wd_ref[cs, :], preferred_element_type=F32)
    out_ref[0] = x + mod[5:6] * acc


def _merge_mlp(x, mod, ya, o0, l0, o1, l1, o2, l2, gates, wa, wb, wo, g2, wu, wd):
    B, S, _ = x.shape
    tm = TM_OUT
    n = S // tm
    const = lambda t: (0, 0)
    prev = lambda t: jnp.maximum(t - 1, 0)
    cur = lambda t: jnp.minimum(t, B * n - 1)
    tspec = lambda w: pl.BlockSpec((1, tm, w), lambda t: (prev(t) // n, prev(t) % n, 0))
    dspec = lambda w: pl.BlockSpec((1, tm, w), lambda t: (cur(t) // n, cur(t) % n, 0))
    cspec = lambda a: pl.BlockSpec((1, 1) + a.shape[2:], lambda t: (cur(t) // n, cur(t) % n, 0, 0))
    wspec = lambda a: pl.BlockSpec(a.shape, const, pipeline_mode=pl.Buffered(1))
    return pl.pallas_call(
        _merge_mlp_kernel,
        grid=(B * n + 1,),
        in_specs=[tspec(D_MODEL), pl.BlockSpec((1, 6, D_MODEL), lambda t: (prev(t) // n, 0, 0)), tspec(DA_V),
                  dspec(DIL_W), dspec(DIL_W), cspec(o1), cspec(l1), cspec(o2), cspec(l2),
                  tspec(2 * D_MODEL), wspec(wa), wspec(wb), wspec(wo),
                  pl.BlockSpec((1, D_MODEL), const), wspec(wu), wspec(wd)],
        out_specs=tspec(D_MODEL),
        out_shape=jax.ShapeDtypeStruct((B, S, D_MODEL), F32),
        scratch_shapes=[pltpu.VMEM((2, DIL_W // LANES, tm, LANES), F32), pltpu.VMEM((tm, DIL_W), BF16)],
        compiler_params=_cparams(("arbitrary",), 56),
        name="merge_mlp",
    )(x, mod, ya, o0, l0, o1, l1, o2, l2, gates, wa, wb, wo, g2, wu, wd)


def _pair_heads(a, b):
    d = a.shape[0]
    return jnp.stack([a.reshape(d, DA_HEADS, HEAD_DIM), b.reshape(d, DA_HEADS, HEAD_DIM)],
                     axis=2).reshape(d, 2 * DA_QK)


def _layer(x, mod, p):
    B, S, _ = x.shape
    qa, ka, va, dil0, dil1, dil2, gates = _inproj(x, mod, p["g1"], p["w_in"], p["gains"], p["gmat"])
    ya = _diff_attention(p["dac"], p["lam"], qa, ka, va, p["da_bias"], p["subln_g"])
    o0, l0 = _dilated_attention(p["dilc"], dil0.reshape(B, 1, S, 3 * DIL_W), p["dil_bias"], 0)
    o1, l1 = _dilated_attention(p["dilc"], dil1, p["dil_bias"], 1)
    o2, l2 = _dilated_attention(p["dilc"], dil2, p["dil_bias"], 2)
    rows = lambda a: a.reshape(B, a.shape[1], -1, LANES)
    return _merge_mlp(x, mod, ya, o0.reshape(B, S, DIL_W), l0.reshape(B, S, DIL_W),
                      rows(o1), rows(l1), rows(o2), rows(l2), gates,
                      p["w_br_a"], p["w_br_b"], p["w_o"], p["g2"], p["w_up"], p["w_down"])


def kernel(x_prompt, x_sample, c_prompt, c_sample, rel_bias, norm1_g, w_ada, b_ada, w_in, qn_a, kn_a,
           lambda_q1, lambda_k1, lambda_q2, lambda_k2, subln_g, qn_b, kn_b, w_br_a, w_br_b, w_o,
           norm2_g, w_up, w_down):
    nbp, nbs = c_prompt.shape[0], c_sample.shape[0]
    pad = (-(nbp + nbs)) % 8
    c_all = jnp.concatenate([c_prompt, c_sample, jnp.zeros((pad, D_MODEL), F32)], axis=0)
    mod = _modulation(c_all, w_ada[0], b_ada[0]).reshape(-1, 6, D_MODEL)

    w = w_in[0]
    q1, q2, k1, k2 = (w[:, n * DA_QK:(n + 1) * DA_QK] for n in range(4))
    w_perm = jnp.concatenate([_pair_heads(q1, q2), _pair_heads(k1, k2), w[:, 4 * DA_QK:]], axis=1)

    scale = HEAD_DIM ** -0.5
    tile8 = lambda v: jnp.tile(v, DIL_W // HEAD_DIM)
    gain_rows = [tile8(qn_a[0]) * (scale * LOG2E)] * 2 + [tile8(kn_a[0])] * 2
    for g in range(N_DIL):
        gain_rows += [tile8(qn_b[0, g]) * (scale * LOG2E), tile8(kn_b[0, g])]
    ids = jnp.arange(NORM_W) // HEAD_DIM
    gmat = jnp.where(ids[:, None] == ids[None, :], 1.0 / HEAD_DIM, 0.0).astype(BF16)

    def logit_reference(q_gain, k_gain, tab):
        qk_bound = HEAD_DIM * scale * jnp.max(jnp.abs(q_gain)) * jnp.max(jnp.abs(k_gain)) * BF16_ROUND_MARGIN
        c_ref = LOG2E * (qk_bound + jnp.max(tab, axis=0)) - DA_HEADROOM
        u_min = LOG2E * (jnp.min(tab, axis=0) - qk_bound) - c_ref
        return c_ref, jnp.all(u_min >= DA_MIN_EXP).astype(F32) * jnp.ones_like(c_ref)

    da_tab = rel_bias[:, :DA_HEADS]
    c_ref, da_ok = logit_reference(qn_a[0], kn_a[0], da_tab)
    dac = jnp.stack([c_ref, c_ref - LOG2E * da_tab[N_BUCKETS // 2 - 1], c_ref - LOG2E * da_tab[N_BUCKETS - 1],
                     da_ok], axis=1)
    dilc = jnp.stack([jnp.stack(logit_reference(
        qn_b[0, g], kn_b[0, g], rel_bias[:, DA_HEADS + g * DIL_HEADS:DA_HEADS + (g + 1) * DIL_HEADS]), axis=1)
        for g in range(N_DIL)], axis=0)

    p = {
        "g1": norm1_g[0].reshape(1, D_MODEL),
        "g2": norm2_g[0].reshape(1, D_MODEL),
        "w_in": w_perm.astype(BF16),
        "gains": jnp.stack(gain_rows, axis=0),
        "gmat": gmat,
        "lam": jnp.stack([lambda_q1[0], lambda_k1[0], lambda_q2[0], lambda_k2[0]], axis=0),
        "subln_g": subln_g[0].reshape(1, DA_VDIM),
        "dac": dac,
        "da_bias": _da_bias_tiles(rel_bias, dac),
        "dilc": dilc,
        "dil_bias": _dil_bias_tiles(rel_bias, dilc),
        "w_br_a": w_br_a[0].astype(BF16),
        "w_br_b": w_br_b[0].astype(BF16),
        "w_o": w_o[0].astype(BF16),
        "w_up": w_up[0].astype(BF16),
        "w_down": w_down[0].astype(BF16),
    }
    y_prompt = _layer(x_prompt, mod[:nbp], p)
    y_sample = _layer(x_sample, mod[nbp:nbp + nbs], p)
    return (y_prompt, y_sample)
```
